```python
import jax, jax.numpy as jnp
from jax import lax
import numpy as np

D_MODEL = 2048
BATCH = 16
SEQ = 256
DEPTH = 2
DEC_BATCH = 4
DEC_SEQ = 4096
PAST_LEN = 512

GRID_W = 64
N_EVEN = (DEPTH + 1) // 2
N_ODD = DEPTH // 2
EPS = 1e-6
CHUNK = 64
F32 = jnp.float32

GLA_WIDTH = 3 * D_MODEL // 4
GLA_HEADS = 6
GLA_DV = GLA_WIDTH // GLA_HEADS
GLA_DK = GLA_DV // 2
GLA_GATE_RANK = 16
GLA_GATE_NORM = 16.0
FNET_WIDTH = D_MODEL - GLA_WIDTH
FNET_GROUPS = 4
FNET_CH = FNET_WIDTH // FNET_GROUPS

ATT_WIDTH = D_MODEL // 2
ATT_HD = 128
ATT_HEADS = ATT_WIDTH // ATT_HD
ATT_KV_HEADS = 2
KV_WIDTH = ATT_KV_HEADS * ATT_HD
AXIS_DIM = ATT_HD // 2
ROPE_THETA = 10000.0
Q_BLOCK = 128
GDN_WIDTH = D_MODEL - ATT_WIDTH
GDN_DK = 128
GDN_DV = 128
GDN_HEADS = GDN_WIDTH // GDN_DV
CONV_W = 3

PEER_HEADS = 8
PEER_NKEYS = 128
PEER_DKEY = 256
PEER_TOPK = 16
N_EXPERTS = PEER_NKEYS ** 2
TOK_BLOCK = 128

EVEN_SPLITS = (GLA_HEADS * GLA_DK, GLA_HEADS * GLA_DK, GLA_WIDTH, GLA_WIDTH, GLA_GATE_RANK, GLA_GATE_RANK, FNET_WIDTH)
EVEN_IN = sum(EVEN_SPLITS)
ODD_SPLITS = (ATT_WIDTH, KV_WIDTH, KV_WIDTH, 3 * GDN_WIDTH, GDN_WIDTH, GDN_HEADS, GDN_HEADS, GDN_HEADS, GDN_HEADS)
ODD_IN = sum(ODD_SPLITS)

kernel_name = 'hybrid_diffusion_gla_fnet_gqa_gdn_peer_step'


def split_cols(x, sizes):
    return jnp.split(x, [int(s) for s in np.cumsum(sizes)[:-1]], axis=-1)


def rmsnorm(x, g):
    xf = x.astype(F32)
    y = xf * lax.rsqrt(jnp.mean(xf * xf, axis=-1, keepdims=True) + EPS)
    return (y * g.astype(F32)).astype(x.dtype)


def l2norm(x):
    xf = x.astype(F32)
    return (xf * lax.rsqrt(jnp.sum(xf * xf, axis=-1, keepdims=True) + EPS)).astype(x.dtype)


def flip(a):
    return a[:, ::-1]


def modulation(cvec, ada_w, ada_b):
    m = (jax.nn.silu(cvec) @ ada_w + ada_b)[:, None, :]
    return jnp.split(m, 6, axis=-1)


def gla_scan(q, k, v, logd, s0):
    B, T, H, _ = q.shape
    n = T // CHUNK

    def ch(a):
        return a.astype(F32).reshape(B, n, CHUNK, H, -1).transpose(1, 0, 3, 2, 4)

    causal = jnp.tril(jnp.ones((CHUNK, CHUNK), bool))

    def step(S, inp):
        qi, ki, vi, gi = inp
        b = jnp.cumsum(gi, axis=-2)
        ref = b[..., CHUNK // 2:CHUNK // 2 + 1, :]
        b_last = b[..., -1:, :]
        a = jnp.einsum('bhik,bhjk->bhij', qi * jnp.exp(b - ref), ki * jnp.exp(ref - b))
        a = jnp.where(causal, a, 0.0)
        o = jnp.einsum('bhik,bhkv->bhiv', qi * jnp.exp(b), S) + jnp.einsum('bhij,bhjv->bhiv', a, vi)
        S = jnp.exp(b_last[..., 0, :])[..., None] * S + jnp.einsum('bhjk,bhjv->bhkv', ki * jnp.exp(b_last - b), vi)
        return S, o

    S, o = lax.scan(step, s0.astype(F32), (ch(q), ch(k), ch(v), ch(logd)))
    return o.transpose(1, 0, 3, 2, 4).reshape(B, T, H, -1).astype(q.dtype), S


def gdn_scan(q, k, v, beta, logd, s0):
    B, T, H, _ = q.shape
    DV = v.shape[-1]
    n = T // CHUNK

    def ch(a):
        return a.astype(F32).reshape(B, n, CHUNK, H, -1).transpose(1, 0, 3, 2, 4)

    idx = jnp.arange(CHUNK)
    incl = idx[:, None] >= idx[None, :]
    strict = idx[:, None] > idx[None, :]
    eye = jnp.eye(CHUNK, dtype=F32)

    def step(S, inp):
        qi, ki, vi, bi, gi = inp
        d = jnp.cumsum(gi, axis=-1)
        L = jnp.exp(jnp.where(incl, d[..., :, None] - d[..., None, :], -jnp.inf))
        kb = ki * bi[..., None]
        A = jnp.where(strict, jnp.einsum('bhik,bhjk->bhij', kb, ki) * L, 0.0)
        rhs = jnp.concatenate([vi * bi[..., None], kb * jnp.exp(d)[..., None]], axis=-1)
        sol = lax.linalg.triangular_solve(eye + A, rhs, left_side=True, lower=True, unit_diagonal=True)
        u, w = sol[..., :DV], sol[..., DV:]
        v_new = u - jnp.einsum('bhik,bhkv->bhiv', w, S)
        att = jnp.einsum('bhik,bhjk->bhij', qi, ki) * L
        o = jnp.einsum('bhik,bhkv->bhiv', qi * jnp.exp(d)[..., None], S) + jnp.einsum('bhij,bhjv->bhiv', att, v_new)
        S = jnp.exp(d[..., -1])[..., None, None] * S + jnp.einsum('bhjk,bhjv->bhkv', ki * jnp.exp(d[..., -1:] - d)[..., None], v_new)
        return S, o

    bc = ch(beta[..., None])[..., 0]
    gc = ch(logd[..., None])[..., 0]
    S, o = lax.scan(step, s0.astype(F32), (ch(q), ch(k), ch(v), bc, gc))
    return o.transpose(1, 0, 3, 2, 4).reshape(B, T, H, DV).astype(q.dtype), S


def fnet_mix(x):
    B, T, _ = x.shape
    xg = x.astype(F32).reshape(B, T, FNET_GROUPS, FNET_CH)
    y = jnp.fft.fft2(xg, axes=(1, 3), norm='ortho').real
    return y.reshape(B, T, FNET_WIDTH).astype(x.dtype)


def axial_rope(x):
    T = x.shape[1]
    rows = T // GRID_W
    row = jnp.repeat(jnp.arange(rows, dtype=F32), GRID_W)
    col = jnp.tile(jnp.arange(GRID_W, dtype=F32), rows)
    freqs = ROPE_THETA ** (-jnp.arange(AXIS_DIM // 2, dtype=F32) * 2.0 / AXIS_DIM)

    def rot(xa, pos):
        ang = pos[:, None] * freqs[None, :]
        cos, sin = jnp.cos(ang)[None, :, None, :], jnp.sin(ang)[None, :, None, :]
        x1, x2 = xa[..., :AXIS_DIM // 2], xa[..., AXIS_DIM // 2:]
        return jnp.concatenate([x1 * cos - x2 * sin, x2 * cos + x1 * sin], axis=-1)

    xf = x.astype(F32)
    return jnp.concatenate([rot(xf[..., :AXIS_DIM], row), rot(xf[..., AXIS_DIM:], col)], axis=-1).astype(x.dtype)


def block_attention(q, k, v):
    B, Tq, H, HD = q.shape
    G = H // ATT_KV_HEADS
    nb = Tq // Q_BLOCK
    qb = q.reshape(B, nb, Q_BLOCK, ATT_KV_HEADS, G, HD).transpose(1, 0, 2, 3, 4, 5)
    kf, vf = k.astype(F32), v.astype(F32)
    scale = HD ** -0.5

    def one(qi):
        s = jnp.einsum('bqkgd,bnkd->bkgqn', qi.astype(F32), kf) * scale
        p = jax.nn.softmax(s, axis=-1)
        return jnp.einsum('bkgqn,bnkd->bqkgd', p, vf).astype(q.dtype)

    o = lax.map(one, qb)
    return o.transpose(1, 0, 2, 3, 4, 5).reshape(B, Tq, H, HD)


def dwconv(x, w):
    C = x.shape[-1]
    return lax.conv_general_dilated(x, w[:, None, :].astype(x.dtype), window_strides=(1,), padding='SAME',
                                    dimension_numbers=('NWC', 'WIO', 'NWC'), feature_group_count=C)


def even_mix(h, w_in, gate_w2, gate_b, out_g, s0_f, s0_b):
    B, T, _ = h.shape
    q, k, v, g, lr_f, lr_b, xf = split_cols(h @ w_in, EVEN_SPLITS)
    q = q.reshape(B, T, GLA_HEADS, GLA_DK) * (GLA_DK ** -0.5)
    k = k.reshape(B, T, GLA_HEADS, GLA_DK)
    v = v.reshape(B, T, GLA_HEADS, GLA_DV)

    def log_decay(lr, w2, b2):
        return (jax.nn.log_sigmoid((lr @ w2 + b2).astype(F32)) / GLA_GATE_NORM).reshape(B, T, GLA_HEADS, GLA_DK)

    o_f, s_f = gla_scan(q, k, v, log_decay(lr_f, gate_w2[0], gate_b[0]), s0_f)
    o_b, s_b = gla_scan(flip(q), flip(k), flip(v), flip(log_decay(lr_b, gate_w2[1], gate_b[1])), s0_b)
    o = rmsnorm(o_f + flip(o_b), out_g) * jax.nn.silu(g.reshape(B, T, GLA_HEADS, GLA_DV))
    return jnp.concatenate([o.reshape(B, T, GLA_WIDTH), fnet_mix(xf)], axis=-1), s_f, s_b


def odd_mix(h, w_in, qk_g, conv_w, a_log, dt_bias, out_g, ctx_k, ctx_v, s0_f, s0_b):
    B, T, _ = h.shape
    qa, ka, va, qkv, z, b_f, b_b, a_f, a_b = split_cols(h @ w_in, ODD_SPLITS)
    qa = rmsnorm(qa.reshape(B, T, ATT_HEADS, ATT_HD), qk_g[0])
    ka = rmsnorm(ka.reshape(B, T, ATT_KV_HEADS, ATT_HD), qk_g[1])
    va = va.reshape(B, T, ATT_KV_HEADS, ATT_HD)
    if ctx_k is None:
        k_all, v_all = ka, va
    else:
        qa, ka = axial_rope(qa), axial_rope(ka)
        k_all = jnp.concatenate([ctx_k.astype(ka.dtype), ka], axis=1)
        v_all = jnp.concatenate([ctx_v.astype(va.dtype), va], axis=1)
    o_att = block_attention(qa, k_all, v_all).reshape(B, T, ATT_WIDTH)

    qkv = jax.nn.silu(dwconv(qkv, conv_w))
    qd, kd, vd = jnp.split(qkv, 3, axis=-1)
    qd = l2norm(qd.reshape(B, T, GDN_HEADS, GDN_DK)) * (GDN_DK ** -0.5)
    kd = l2norm(kd.reshape(B, T, GDN_HEADS, GDN_DK))
    vd = vd.reshape(B, T, GDN_HEADS, GDN_DV)

    def log_decay(a, j):
        return -jnp.exp(a_log[j].astype(F32)) * jax.nn.softplus(a.astype(F32) + dt_bias[j].astype(F32))

    beta_f = jax.nn.sigmoid(b_f.astype(F32))
    beta_b = jax.nn.sigmoid(b_b.astype(F32))
    o_f, s_f = gdn_scan(qd, kd, vd, beta_f, log_decay(a_f, 0), s0_f)
    o_b, s_b = gdn_scan(flip(qd), flip(kd), flip(vd), flip(beta_b), flip(log_decay(a_b, 1)), s0_b)
    o_d = rmsnorm(o_f + flip(o_b), out_g) * jax.nn.silu(z.reshape(B, T, GDN_HEADS, GDN_DV))
    return jnp.concatenate([o_att, o_d.reshape(B, T, GDN_WIDTH)], axis=-1), ka, va, s_f, s_b


def peer(h, wq, keys, U, V):
    B, T, D = h.shape
    half = PEER_DKEY // 2
    hb = h.reshape((B * T) // TOK_BLOCK, TOK_BLOCK, D)
    kf = keys.astype(F32)

    def one(x):
        qry = (x @ wq).reshape(TOK_BLOCK, PEER_HEADS, 2, half).astype(F32)
        s = jnp.einsum('nhpd,hpkd->nhpk', qry, kf)
        s1, i1 = lax.top_k(s[:, :, 0], PEER_TOPK)
        s2, i2 = lax.top_k(s[:, :, 1], PEER_TOPK)
        cand = (s1[..., :, None] + s2[..., None, :]).reshape(TOK_BLOCK, PEER_HEADS, PEER_TOPK * PEER_TOPK)
        cid = (i1[..., :, None] * PEER_NKEYS + i2[..., None, :]).reshape(TOK_BLOCK, PEER_HEADS, PEER_TOPK * PEER_TOPK)
        top_s, top_i = lax.top_k(cand, PEER_TOPK)
        eid = jnp.take_along_axis(cid, top_i, axis=-1)
        gate = jax.nn.softmax(top_s, axis=-1)
        a = jnp.einsum('nd,nhkd->nhk', x, U[eid])
        coef = (gate * jax.nn.gelu(a.astype(F32))).astype(x.dtype)
        return jnp.einsum('nhk,nhkd->nd', coef, V[eid])

    return lax.map(one, hb).reshape(B, T, D)


def setup_inputs(seed: int = 0) -> dict:
    key = jax.random.key(seed)
    ks = jax.random.split(key, 28)
    nrm = jax.random.normal
    dt = jnp.exp(jax.random.uniform(ks[20], (N_ODD, 2, GDN_HEADS), minval=float(np.log(1e-3)), maxval=float(np.log(1e-1))))
    return {
        'x_prompt': nrm(ks[0], (BATCH, SEQ, D_MODEL), F32),
        'x_sample': nrm(ks[1], (DEC_BATCH, DEC_SEQ, D_MODEL), F32),
        'cache_k': nrm(ks[2], (DEC_BATCH, N_ODD, PAST_LEN, ATT_KV_HEADS, ATT_HD), F32),
        'cache_v': nrm(ks[3], (DEC_BATCH, N_ODD, PAST_LEN, ATT_KV_HEADS, ATT_HD), F32),
        'state_gla': 0.5 * nrm(ks[4], (DEC_BATCH, N_EVEN, 2, GLA_HEADS, GLA_DK, GLA_DV), F32),
        'state_delta': 0.1 * nrm(ks[5], (DEC_BATCH, N_ODD, 2, GDN_HEADS, GDN_DK, GDN_DV), F32),
        'c': nrm(ks[6], (DEC_BATCH, D_MODEL), F32),
        'c_ctx': nrm(ks[7], (D_MODEL,), F32),
        'ada_w': 0.2 * D_MODEL ** -0.5 * nrm(ks[8], (DEPTH, D_MODEL, 6 * D_MODEL), F32),
        'ada_b': 0.02 * nrm(ks[9], (DEPTH, 6 * D_MODEL), F32),
        'norm_g': 1.0 + 0.05 * nrm(ks[10], (DEPTH, 4, D_MODEL), F32),
        'w_out': D_MODEL ** -0.5 * nrm(ks[11], (DEPTH, D_MODEL, D_MODEL), F32),
        'even_w_in': D_MODEL ** -0.5 * nrm(ks[12], (N_EVEN, D_MODEL, EVEN_IN), F32),
        'gla_gate_w2': GLA_GATE_RANK ** -0.5 * nrm(ks[13], (N_EVEN, 2, GLA_GATE_RANK, GLA_HEADS * GLA_DK), F32),
        'gla_gate_b': 0.1 * nrm(ks[14], (N_EVEN, 2, GLA_HEADS * GLA_DK), F32),
        'gla_out_norm': 1.0 + 0.05 * nrm(ks[15], (N_EVEN, GLA_DV), F32),
        'odd_w_in': D_MODEL ** -0.5 * nrm(ks[16], (N_ODD, D_MODEL, ODD_IN), F32),
        'qk_norm': 1.0 + 0.05 * nrm(ks[17], (N_ODD, 2, ATT_HD), F32),
        'conv_w': CONV_W ** -0.5 * nrm(ks[18], (N_ODD, CONV_W, 3 * GDN_WIDTH), F32),
        'gdn_a_log': jnp.log(jax.random.uniform(ks[19], (N_ODD, 2, GDN_HEADS), minval=1.0, maxval=16.0)),
        'gdn_dt_bias': dt + jnp.log(-jnp.expm1(-dt)),
        'gdn_out_norm': 1.0 + 0.05 * nrm(ks[21], (N_ODD, GDN_DV), F32),
        'peer_wq': D_MODEL ** -0.5 * nrm(ks[22], (DEPTH, D_MODEL, PEER_HEADS * PEER_DKEY), F32),
        'peer_keys': (PEER_DKEY // 2) ** -0.5 * nrm(ks[23], (DEPTH, PEER_HEADS, 2, PEER_NKEYS, PEER_DKEY // 2), F32),
        'peer_u': D_MODEL ** -0.5 * nrm(ks[24], (DEPTH, N_EXPERTS, D_MODEL), F32),
        'peer_v': PEER_HEADS ** -0.5 * nrm(ks[25], (DEPTH, N_EXPERTS, D_MODEL), F32),
    }


def reference(x_prompt, x_sample, cache_k, cache_v, state_gla, state_delta, c, c_ctx,
              ada_w, ada_b, norm_g, w_out, even_w_in, gla_gate_w2, gla_gate_b, gla_out_norm,
              odd_w_in, qk_norm, conv_w, gdn_a_log, gdn_dt_bias, gdn_out_norm,
              peer_wq, peer_keys, peer_u, peer_v):
    xp, xs = x_prompt, x_sample
    new_gla, new_delta, new_k, new_v = [], [], [], []
    for l in range(DEPTH):
        i = l // 2
        sh1p, sc1p, g1p, sh2p, sc2p, g2p = modulation(c_ctx[None, :], ada_w[l], ada_b[l])
        sh1s, sc1s, g1s, sh2s, sc2s, g2s = modulation(c, ada_w[l], ada_b[l])
        hp = rmsnorm(xp, norm_g[l, 0]) * (1 + sc1p) + sh1p
        hs = rmsnorm(xs, norm_g[l, 0]) * (1 + sc1s) + sh1s
        if l % 2 == 0:
            zero = jnp.zeros((xp.shape[0], GLA_HEADS, GLA_DK, GLA_DV), F32)
            mp, sf, sb = even_mix(hp, even_w_in[i], gla_gate_w2[i], gla_gate_b[i], gla_out_norm[i], zero, zero)
            ms, _, _ = even_mix(hs, even_w_in[i], gla_gate_w2[i], gla_gate_b[i], gla_out_norm[i],
                                state_gla[:, i, 0], state_gla[:, i, 1])
            new_gla.append(jnp.stack([sf, sb], axis=1))
        else:
            zero = jnp.zeros((xp.shape[0], GDN_HEADS, GDN_DK, GDN_DV), F32)
            mp, kp, vp, sf, sb = odd_mix(hp, odd_w_in[i], qk_norm[i], conv_w[i], gdn_a_log[i], gdn_dt_bias[i],
                                         gdn_out_norm[i], None, None, zero, zero)
            ms, _, _, _, _ = odd_mix(hs, odd_w_in[i], qk_norm[i], conv_w[i], gdn_a_log[i], gdn_dt_bias[i],
                                     gdn_out_norm[i], cache_k[:, i], cache_v[:, i],
                                     state_delta[:, i, 0], state_delta[:, i, 1])
            new_k.append(kp)
            new_v.append(vp)
            new_delta.append(jnp.stack([sf, sb], axis=1))
        xp = xp + g1p * rmsnorm(mp @ w_out[l], norm_g[l, 1])
        xs = xs + g1s * rmsnorm(ms @ w_out[l], norm_g[l, 1])
        hp = rmsnorm(xp, norm_g[l, 2]) * (1 + sc2p) + sh2p
        hs = rmsnorm(xs, norm_g[l, 2]) * (1 + sc2s) + sh2s
        xp = xp + g2p * rmsnorm(peer(hp, peer_wq[l], peer_keys[l], peer_u[l], peer_v[l]), norm_g[l, 3])
        xs = xs + g2s * rmsnorm(peer(hs, peer_wq[l], peer_keys[l], peer_u[l], peer_v[l]), norm_g[l, 3])
    new_cache_k = jnp.stack(new_k, axis=1)
    new_cache_v = jnp.stack(new_v, axis=1)
    new_state_gla = jnp.stack(new_gla, axis=1)
    new_state_delta = jnp.stack(new_delta, axis=1)
    return (xp, xs, new_cache_k, new_cache_v, new_state_gla, new_state_delta)
```

```python
import functools

import numpy as np
import jax
import jax.numpy as jnp
from jax import lax
from jax.experimental import pallas as pl
from jax.experimental.pallas import tpu as pltpu

F32 = jnp.float32
BF16 = jnp.bfloat16
HIGHEST = lax.Precision.HIGHEST

D_MODEL = 2048
EPS = 1e-6
CHUNK = 64

GLA_HEADS = 6
GLA_DK = 128
GLA_DV = 256
GLA_WIDTH = GLA_HEADS * GLA_DV
GLA_GATE_RANK = 16
GLA_GATE_NORM = 16.0
FNET_WIDTH = D_MODEL - GLA_WIDTH
FNET_GROUPS = 4
FNET_CH = FNET_WIDTH // FNET_GROUPS

ATT_HD = 128
ATT_HEADS = 8
ATT_KV_HEADS = 2
ATT_GROUP = ATT_HEADS // ATT_KV_HEADS
ATT_WIDTH = ATT_HEADS * ATT_HD
KV_WIDTH = ATT_KV_HEADS * ATT_HD
AXIS_DIM = ATT_HD // 2
ROPE_THETA = 10000.0
GRID_W = 64
GDN_HEADS = 8
GDN_DK = 128
GDN_DV = 128
GDN_WIDTH = GDN_HEADS * GDN_DV
CONV_W = 3

PEER_HEADS = 8
PEER_NKEYS = 128
PEER_HALF = 128
PEER_TOPK = 16
N_EXPERTS = PEER_NKEYS ** 2

EVEN_SPLITS = (GLA_HEADS * GLA_DK, GLA_HEADS * GLA_DK, GLA_WIDTH, GLA_WIDTH, GLA_GATE_RANK, GLA_GATE_RANK, FNET_WIDTH)
ODD_SPLITS = (ATT_WIDTH, KV_WIDTH, KV_WIDTH, 3 * GDN_WIDTH, GDN_WIDTH, GDN_HEADS, GDN_HEADS, GDN_HEADS, GDN_HEADS)

VMEM_LIMIT_BYTES = 56 * 1024 * 1024
NEG_INF = float("-inf")

NT_DIMS = (((1,), (1,)), ((), ()))


def _params(*semantics):
    return pltpu.CompilerParams(dimension_semantics=semantics, vmem_limit_bytes=VMEM_LIMIT_BYTES)


def _split_cols(x, sizes):
    return jnp.split(x, [int(s) for s in np.cumsum(sizes)[:-1]], axis=-1)


def _round_up(n, m):
    return -(-n // m) * m


def _mm_kernel(x_ref, w_ref, o_ref, acc_ref):
    k = pl.program_id(2)

    @pl.when(k == 0)
    def _():
        acc_ref[...] = jnp.zeros_like(acc_ref)

    acc_ref[...] += jnp.dot(x_ref[...].astype(BF16), w_ref[...].astype(BF16), preferred_element_type=F32)

    @pl.when(k == pl.num_programs(2) - 1)
    def _():
        o_ref[...] = acc_ref[...]


def _pick_tile(n, pref):
    for t in (pref, 1024, 512, 256, 128):
        if t <= pref and n % t == 0:
            return t
    return n


def matmul(x, w, *, tm=1024, tn=512, tk=2048):
    m, kdim = x.shape
    n = w.shape[1]
    m_pad = _round_up(m, 8)
    n_pad = _round_up(n, 256)
    if m_pad != m:
        x = jnp.pad(x, ((0, m_pad - m), (0, 0)))
    if n_pad != n:
        w = jnp.pad(w, ((0, 0), (0, n_pad - n)))
    tm = _pick_tile(m_pad, tm)
    tn = _pick_tile(n_pad, tn)
    tk = _pick_tile(kdim, tk)
    out = pl.pallas_call(
        _mm_kernel,
        grid=(m_pad // tm, n_pad // tn, kdim // tk),
        in_specs=[pl.BlockSpec((tm, tk), lambda i, j, k: (i, k)),
                  pl.BlockSpec((tk, tn), lambda i, j, k: (k, j))],
        out_specs=pl.BlockSpec((tm, tn), lambda i, j, k: (i, j)),
        out_shape=jax.ShapeDtypeStruct((m_pad, n_pad), F32),
        scratch_shapes=[pltpu.VMEM((tm, tn), F32)],
        compiler_params=_params("parallel", "parallel", "arbitrary"),
        name="matmul",
    )(x, w)
    return out[:m, :n]


def _split_hi_lo(x):
    hi = x.astype(BF16)
    lo = (x - hi.astype(F32)).astype(BF16)
    return hi, lo


def _gla_kernel(q_ref, k_ref, v_ref, lr_ref, w2_ref, gb_ref, s0_ref, o_ref, sfin_ref, st_scr):
    d = pl.program_id(1)
    c = pl.program_id(2)
    rev = d == 1

    @pl.when(c == 0)
    def _():
        st_scr[...] = s0_ref[0, 0]

    ii = lax.broadcasted_iota(jnp.int32, (CHUNK, CHUNK), 0)
    jj = lax.broadcasted_iota(jnp.int32, (CHUNK, CHUNK), 1)
    incl = (jj - ii) * (1 - 2 * d) <= 0
    tri = jnp.where(incl, 1.0, 0.0).astype(BF16)

    pre = jnp.dot(lr_ref[0, 0].astype(BF16), w2_ref[0].astype(BF16), preferred_element_type=F32) + gb_ref[0]
    logd = (jnp.minimum(pre, 0.0) - jnp.log1p(jnp.exp(-jnp.abs(pre)))) / GLA_GATE_NORM
    g_hi, g_lo = _split_hi_lo(logd)
    b = jnp.dot(tri, g_hi, preferred_element_type=F32) + jnp.dot(tri, g_lo, preferred_element_type=F32)
    mid = CHUNK // 2
    b_ref = jnp.where(rev, b[mid - 1:mid], b[mid:mid + 1])
    b_last = jnp.where(rev, b[0:1], b[CHUNK - 1:CHUNK])
    e_q = jnp.exp(b - b_ref)
    e_k = jnp.exp(b_ref - b)
    e_b = jnp.exp(b)
    e_kl = jnp.exp(b_last - b)
    e_l = jnp.exp(b_last)

    for h in range(GLA_HEADS):
        ks = slice(h * GLA_DK, (h + 1) * GLA_DK)
        vs = slice(h * GLA_DV, (h + 1) * GLA_DV)
        qi = q_ref[0, :, ks] * (GLA_DK ** -0.5)
        ki = k_ref[0, :, ks]
        vi = v_ref[0, :, vs].astype(BF16)
        a = lax.dot_general((qi * e_q[:, ks]).astype(BF16), (ki * e_k[:, ks]).astype(BF16), NT_DIMS,
                            preferred_element_type=F32)
        a = jnp.where(incl, a, 0.0)
        st = st_scr[h]
        o = lax.dot_general((qi * e_b[:, ks]).astype(BF16), st.astype(BF16), NT_DIMS,
                            preferred_element_type=F32)
        o = o + jnp.dot(a.astype(BF16), vi, preferred_element_type=F32)
        o_ref[0, 0, :, vs] = o
        kd = (ki * e_kl[:, ks]).astype(BF16)
        st_scr[h] = st * e_l[:, ks] + jnp.dot(vi.T, kd, preferred_element_type=F32)

    @pl.when(c == pl.num_programs(2) - 1)
    def _():
        sfin_ref[0, 0] = st_scr[...]


def gla_scan(q, k, v, lr, w2, gb, s0):
    bsz, t, _ = q.shape
    n = t // CHUNK
    s0t = jnp.swapaxes(s0, -1, -2)

    def tmap(b, d, c):
        return (b, c + d * (n - 1 - 2 * c), 0)

    o, sfin = pl.pallas_call(
        _gla_kernel,
        grid=(bsz, 2, n),
        in_specs=[pl.BlockSpec((1, CHUNK, GLA_HEADS * GLA_DK), tmap),
                  pl.BlockSpec((1, CHUNK, GLA_HEADS * GLA_DK), tmap),
                  pl.BlockSpec((1, CHUNK, GLA_WIDTH), tmap),
                  pl.BlockSpec((1, 1, CHUNK, GLA_GATE_RANK), lambda b, d, c: (d, b, c + d * (n - 1 - 2 * c), 0)),
                  pl.BlockSpec((1, GLA_GATE_RANK, GLA_HEADS * GLA_DK), lambda b, d, c: (d, 0, 0)),
                  pl.BlockSpec((1, 1, GLA_HEADS * GLA_DK), lambda b, d, c: (d, 0, 0)),
                  pl.BlockSpec((1, 1, GLA_HEADS, GLA_DV, GLA_DK), lambda b, d, c: (b, d, 0, 0, 0))],
        out_specs=[pl.BlockSpec((1, 1, CHUNK, GLA_WIDTH), lambda b, d, c: (d, b, c + d * (n - 1 - 2 * c), 0)),
                   pl.BlockSpec((1, 1, GLA_HEADS, GLA_DV, GLA_DK), lambda b, d, c: (b, d, 0, 0, 0))],
        out_shape=[jax.ShapeDtypeStruct((2, bsz, t, GLA_WIDTH), F32),
                   jax.ShapeDtypeStruct((bsz, 2, GLA_HEADS, GLA_DV, GLA_DK), F32)],
        scratch_shapes=[pltpu.VMEM((GLA_HEADS, GLA_DV, GLA_DK), F32)],
        compiler_params=_params("parallel", "arbitrary", "arbitrary"),
        name="gla_scan",
    )(q, k, v, lr, w2, gb, s0t)
    return o, jnp.swapaxes(sfin, -1, -2)


def _gdn_kernel(q_ref, k_ref, v_ref, beta_ref, g_ref, gt_ref, s0_ref, o_ref, sfin_ref, s_scr):
    d = pl.program_id(1)
    c = pl.program_id(2)
    rev = d == 1

    @pl.when(c == 0)
    def _():
        s_scr[...] = s0_ref[0, 0]

    ii = lax.broadcasted_iota(jnp.int32, (CHUNK, CHUNK), 0)
    jj = lax.broadcasted_iota(jnp.int32, (CHUNK, CHUNK), 1)
    order = (jj - ii) * (1 - 2 * d)
    incl = order <= 0
    strict = order < 0
    tri = jnp.where(incl, 1.0, 0.0).astype(BF16)
    eye = jnp.where(ii == jj, 1.0, 0.0).astype(F32)

    g_hi, g_lo = _split_hi_lo(g_ref[0, 0])
    dcol = jnp.dot(tri, g_hi, preferred_element_type=F32) + jnp.dot(tri, g_lo, preferred_element_type=F32)
    gt_hi, gt_lo = _split_hi_lo(gt_ref[0, 0, 0])
    drow = (lax.dot_general(gt_hi, tri, NT_DIMS, preferred_element_type=F32)
            + lax.dot_general(gt_lo, tri, NT_DIMS, preferred_element_type=F32))
    beta = beta_ref[0, 0]

    for h in range(GDN_HEADS):
        hs = slice(h * GDN_DK, (h + 1) * GDN_DK)
        qi = q_ref[0, :, hs]
        ki = k_ref[0, :, hs]
        vi = v_ref[0, :, hs]
        dc = dcol[:, h:h + 1]
        dr = drow[h:h + 1, :]
        bc = beta[:, h:h + 1]
        decay = jnp.exp(jnp.where(incl, dc - dr, NEG_INF))
        kb = ki * bc
        kbf = ki.astype(BF16)
        a = lax.dot_general(kb.astype(BF16), kbf, NT_DIMS, preferred_element_type=F32) * decay
        a = jnp.where(strict, a, 0.0)
        tinv = eye - a
        p = a
        for _ in range(5):
            p = jnp.dot(p, p, precision=HIGHEST, preferred_element_type=F32)
            tinv = tinv + jnp.dot(tinv, p, precision=HIGHEST, preferred_element_type=F32)
        e_d = jnp.exp(dc)
        rhs = jnp.concatenate([vi * bc, kb * e_d], axis=-1)
        sol = jnp.dot(tinv, rhs, precision=HIGHEST, preferred_element_type=F32)
        u = sol[:, :GDN_DV]
        w = sol[:, GDN_DV:]
        s = s_scr[h]
        sb = s.astype(BF16)
        v_new = u - jnp.dot(w.astype(BF16), sb, preferred_element_type=F32)
        att = lax.dot_general(qi.astype(BF16), kbf, NT_DIMS, preferred_element_type=F32) * decay
        o = jnp.dot((qi * e_d).astype(BF16), sb, preferred_element_type=F32)
        vnb = v_new.astype(BF16)
        o = o + jnp.dot(att.astype(BF16), vnb, preferred_element_type=F32)
        o_ref[0, 0, :, hs] = o
        d_last = jnp.where(rev, dc[0:1], dc[CHUNK - 1:CHUNK])
        kd = (ki * jnp.exp(d_last - dc)).astype(BF16)
        s_scr[h] = jnp.exp(d_last) * s + jnp.dot(kd.T, vnb, preferred_element_type=F32)

    @pl.when(c == pl.num_programs(2) - 1)
    def _():
        sfin_ref[0, 0] = s_scr[...]


def gdn_scan(q, k, v, beta, logd, s0):
    bsz, t, _ = q.shape
    n = t // CHUNK
    logd_t = jnp.swapaxes(logd.reshape(2, bsz, n, CHUNK, GDN_HEADS), -1, -2)

    def tmap(b, d, c):
        return (b, c + d * (n - 1 - 2 * c), 0)

    def dmap(b, d, c):
        return (d, b, c + d * (n - 1 - 2 * c), 0)

    o, sfin = pl.pallas_call(
        _gdn_kernel,
        grid=(bsz, 2, n),
        in_specs=[pl.BlockSpec((1, CHUNK, GDN_WIDTH), tmap),
                  pl.BlockSpec((1, CHUNK, GDN_WIDTH), tmap),
                  pl.BlockSpec((1, CHUNK, GDN_WIDTH), tmap),
                  pl.BlockSpec((1, 1, CHUNK, GDN_HEADS), dmap),
                  pl.BlockSpec((1, 1, CHUNK, GDN_HEADS), dmap),
                  pl.BlockSpec((1, 1, 1, GDN_HEADS, CHUNK), lambda b, d, c: (d, b, c + d * (n - 1 - 2 * c), 0, 0)),
                  pl.BlockSpec((1, 1, GDN_HEADS, GDN_DK, GDN_DV), lambda b, d, c: (b, d, 0, 0, 0))],
        out_specs=[pl.BlockSpec((1, 1, CHUNK, GDN_WIDTH), dmap),
                   pl.BlockSpec((1, 1, GDN_HEADS, GDN_DK, GDN_DV), lambda b, d, c: (b, d, 0, 0, 0))],
        out_shape=[jax.ShapeDtypeStruct((2, bsz, t, GDN_WIDTH), F32),
                   jax.ShapeDtypeStruct((bsz, 2, GDN_HEADS, GDN_DK, GDN_DV), F32)],
        scratch_shapes=[pltpu.VMEM((GDN_HEADS, GDN_DK, GDN_DV), F32)],
        compiler_params=_params("parallel", "arbitrary", "arbitrary"),
        name="gdn_scan",
    )(q, k, v, beta, logd, logd_t, s0)
    return o, sfin


def _attn_kernel(q_ref, k_ref, v_ref, o_ref):
    k = k_ref[0]
    v = v_ref[0]
    for g in range(ATT_GROUP):
        gs = slice(g * ATT_HD, (g + 1) * ATT_HD)
        s = lax.dot_general(q_ref[0, :, gs], k, NT_DIMS, preferred_element_type=F32) * (ATT_HD ** -0.5)
        m = jnp.max(s, axis=-1, keepdims=True)
        p = jnp.exp(s - m)
        l = jnp.sum(p, axis=-1, keepdims=True)
        o = jnp.dot(p.astype(BF16), v, preferred_element_type=F32)
        o_ref[0, :, gs] = o / l


def attention(q, k, v, *, tq=256):
    bsz, t_q, _ = q.shape
    t_k = k.shape[1]
    tq = min(tq, t_q)
    gw = ATT_GROUP * ATT_HD
    return pl.pallas_call(
        _attn_kernel,
        grid=(bsz, ATT_KV_HEADS, t_q // tq),
        in_specs=[pl.BlockSpec((1, tq, gw), lambda b, h, i: (b, i, h)),
                  pl.BlockSpec((1, t_k, ATT_HD), lambda b, h, i: (b, 0, h)),
                  pl.BlockSpec((1, t_k, ATT_HD), lambda b, h, i: (b, 0, h))],
        out_specs=pl.BlockSpec((1, tq, gw), lambda b, h, i: (b, i, h)),
        out_shape=jax.ShapeDtypeStruct((bsz, t_q, ATT_WIDTH), F32),
        compiler_params=_params("parallel", "parallel", "parallel"),
        name="attention",
    )(q, k, v)


PEER_TB = 512
PEER_ET = 1024
PEER_IT = PEER_ET // PEER_NKEYS
PEER_PAIRS = tuple((a, b) for a in range(PEER_TOPK) for b in range(PEER_TOPK) if (a + 1) * (b + 1) <= PEER_TOPK)


def _top_values(x, count):
    vals = []
    for _ in range(count):
        m = jnp.max(x, axis=0, keepdims=True)
        vals.append(m)
        x = jnp.where(x == m, NEG_INF, x)
    return vals


PEER_CAND_ROWS = _round_up(len(PEER_PAIRS), 8)


def _peer_route_kernel(qt_ref, keys_ref, p1_ref, e2_ref, tau_ref, cand_scr):
    def top_candidates(rows):
        cand_scr[PEER_CAND_ROWS - 8:, :] = jnp.full((8, cand_scr.shape[1]), NEG_INF, F32)
        for r, row in enumerate(rows):
            cand_scr[r:r + 1, :] = row
        return _top_values(cand_scr[...], PEER_TOPK)

    for h in range(PEER_HEADS):
        e = []
        tops = []
        for p in range(2):
            r0 = (h * 2 + p) * PEER_HALF
            s = jnp.dot(keys_ref[h, p], qt_ref[r0:r0 + PEER_HALF, :], precision=HIGHEST,
                        preferred_element_type=F32)
            top = _top_values(s, PEER_TOPK)
            e.append(jnp.exp(s - top[0]))
            tops.append([jnp.exp(t - top[0]) for t in top])
        z = sum(top_candidates([tops[0][a] * tops[1][b] for a, b in PEER_PAIRS]))
        zinv = 1.0 / z
        p1 = e[0] * zinv
        tau = top_candidates([(tops[0][a] * zinv) * tops[1][b] for a, b in PEER_PAIRS])[-1]
        p1_ref[h] = p1
        e2_ref[h] = e[1]
        tau_ref[h] = jnp.broadcast_to(tau, (8, tau.shape[1]))


def _gelu_tanh(x):
    return 0.5 * x * (1.0 + jnp.tanh(np.sqrt(2.0 / np.pi).astype(np.float32) * (x + 0.044715 * (x * x * x))))


def _peer_dense_kernel(ht_ref, u_ref, vt_ref, p1_ref, e2_ref, tau_ref, o_ref, coef_scr):
    et = pl.program_id(1)

    @pl.when(et == 0)
    def _():
        o_ref[...] = jnp.zeros_like(o_ref)

    act = _gelu_tanh(jnp.dot(u_ref[...], ht_ref[...], preferred_element_type=F32))
    tb = ht_ref.shape[1]
    for ts in range(tb // 128):
        ls = slice(ts * 128, (ts + 1) * 128)
        for i in range(PEER_IT):
            gate = jnp.zeros((PEER_NKEYS, 128), F32)
            for h in range(PEER_HEADS):
                w = e2_ref[h, :, ls] * p1_ref[h, i:i + 1, ls]
                gate = gate + jnp.where(w >= tau_ref[h, 0:1, ls], w, 0.0)
            rs = slice(i * PEER_NKEYS, (i + 1) * PEER_NKEYS)
            coef_scr[rs, ls] = (gate * act[rs, ls]).astype(BF16)
    o_ref[...] += jnp.dot(vt_ref[...], coef_scr[...], preferred_element_type=F32)


def peer(ht, wq_t, keys, u, v_t):
    d, n = ht.shape
    qt = matmul(wq_t, ht)
    nb = n // PEER_TB
    p1, e2, tau = pl.pallas_call(
        _peer_route_kernel,
        grid=(nb,),
        in_specs=[pl.BlockSpec((PEER_HEADS * 2 * PEER_HALF, PEER_TB), lambda i: (0, i)),
                  pl.BlockSpec((PEER_HEADS, 2, PEER_NKEYS, PEER_HALF), lambda i: (0, 0, 0, 0))],
        out_specs=[pl.BlockSpec((PEER_HEADS, PEER_NKEYS, PEER_TB), lambda i: (0, 0, i)),
                   pl.BlockSpec((PEER_HEADS, PEER_NKEYS, PEER_TB), lambda i: (0, 0, i)),
                   pl.BlockSpec((PEER_HEADS, 8, PEER_TB), lambda i: (0, 0, i))],
        out_shape=[jax.ShapeDtypeStruct((PEER_HEADS, PEER_NKEYS, n), F32),
                   jax.ShapeDtypeStruct((PEER_HEADS, PEER_NKEYS, n), F32),
                   jax.ShapeDtypeStruct((PEER_HEADS, 8, n), F32)],
        scratch_shapes=[pltpu.VMEM((PEER_CAND_ROWS, PEER_TB), F32)],
        compiler_params=_params("parallel"),
        name="peer_route",
    )(qt, keys)
    return pl.pallas_call(
        _peer_dense_kernel,
        grid=(nb, N_EXPERTS // PEER_ET),
        in_specs=[pl.BlockSpec((d, PEER_TB), lambda i, e: (0, i)),
                  pl.BlockSpec((PEER_ET, d), lambda i, e: (e, 0)),
                  pl.BlockSpec((d, PEER_ET), lambda i, e: (0, e)),
                  pl.BlockSpec((PEER_HEADS, PEER_IT, PEER_TB), lambda i, e: (0, e, i)),
                  pl.BlockSpec((PEER_HEADS, PEER_NKEYS, PEER_TB), lambda i, e: (0, 0, i)),
                  pl.BlockSpec((PEER_HEADS, 8, PEER_TB), lambda i, e: (0, 0, i))],
        out_specs=pl.BlockSpec((d, PEER_TB), lambda i, e: (0, i)),
        out_shape=jax.ShapeDtypeStruct((d, n), F32),
        scratch_shapes=[pltpu.VMEM((PEER_ET, PEER_TB), BF16)],
        compiler_params=_params("parallel", "arbitrary"),
        name="peer_dense",
    )(ht, u, v_t, p1, e2, tau)


def _rmsnorm(x, g):
    return x * lax.rsqrt(jnp.mean(x * x, axis=-1, keepdims=True) + EPS) * g


def _l2norm(x):
    return x * lax.rsqrt(jnp.sum(x * x, axis=-1, keepdims=True) + EPS)


def _dft_tables(t):
    idx = jnp.arange(t, dtype=jnp.int32)
    ang = ((idx[:, None] * idx[None, :]) % t).astype(F32) * (2.0 * np.pi / t)
    return jnp.cos(ang), jnp.sin(ang)


def fnet_mix(xf, n_ctx_seq, ctx_len):
    nseg, seg_len, _ = xf.shape
    cc, sc = _dft_tables(FNET_CH)
    eye = jnp.eye(FNET_GROUPS, dtype=F32)
    w_ch = jnp.concatenate([jnp.kron(eye, cc), jnp.kron(eye, sc)], axis=1)
    z = matmul(xf.reshape(nseg * seg_len, FNET_WIDTH), w_ch).reshape(nseg, seg_len, 2 * FNET_WIDTH)

    def seq_dft(zz, t):
        bsz = zz.shape[0]
        ct, st = _dft_tables(t)
        scale = (t * FNET_CH) ** -0.5
        w_t = jnp.concatenate([ct, -st], axis=1) * scale
        stacked = jnp.concatenate([zz[..., :FNET_WIDTH], zz[..., FNET_WIDTH:]], axis=1)
        rhs = jnp.swapaxes(stacked, 0, 1).reshape(2 * t, bsz * FNET_WIDTH)
        y = matmul(w_t, rhs)
        return jnp.swapaxes(y.reshape(t, bsz, FNET_WIDTH), 0, 1)

    y_ctx = seq_dft(z[0].reshape(n_ctx_seq, ctx_len, 2 * FNET_WIDTH), ctx_len).reshape(1, seg_len, FNET_WIDTH)
    y_lat = seq_dft(z[1:], seg_len)
    return jnp.concatenate([y_ctx, y_lat], axis=0)


def _axial_rope(x):
    t = x.shape[1]
    rows = t // GRID_W
    row = jnp.repeat(jnp.arange(rows, dtype=F32), GRID_W)
    col = jnp.tile(jnp.arange(GRID_W, dtype=F32), rows)
    freqs = ROPE_THETA ** (-jnp.arange(AXIS_DIM // 2, dtype=F32) * 2.0 / AXIS_DIM)

    def rot(xa, pos):
        ang = pos[:, None] * freqs[None, :]
        cos, sin = jnp.cos(ang)[None, :, None, :], jnp.sin(ang)[None, :, None, :]
        x1, x2 = xa[..., :AXIS_DIM // 2], xa[..., AXIS_DIM // 2:]
        return jnp.concatenate([x1 * cos - x2 * sin, x2 * cos + x1 * sin], axis=-1)

    return jnp.concatenate([rot(x[..., :AXIS_DIM], row), rot(x[..., AXIS_DIM:], col)], axis=-1)


def _dwconv(x, w):
    xp = jnp.pad(x, ((0, 0), (1, 1), (0, 0)))
    return xp[:, :-2] * w[0] + xp[:, 1:-1] * w[1] + xp[:, 2:] * w[2]


def _even_mix(h, n_ctx, ctx_len, w_in, gate_w2, gate_b, out_g, state):
    nseg, seg_len, _ = h.shape
    proj = matmul(h.reshape(nseg * seg_len, D_MODEL), w_in).reshape(nseg, seg_len, -1)
    q, k, v, g, lr_f, lr_b, xf = _split_cols(proj, EVEN_SPLITS)
    lr = jnp.stack([lr_f, lr_b])
    gb = gate_b[:, None, :]

    def run(sl, bsz, t, s0):
        def r(a):
            return a[sl].reshape(bsz, t, a.shape[-1])
        o, sfin = gla_scan(r(q), r(k), r(v), lr[:, sl].reshape(2, bsz, t, GLA_GATE_RANK), gate_w2, gb, s0)
        return (o[0] + o[1]).reshape(-1, seg_len, GLA_WIDTH), sfin

    zero = jnp.zeros((n_ctx, 2, GLA_HEADS, GLA_DK, GLA_DV), F32)
    o_ctx, s_ctx = run(slice(0, 1), n_ctx, ctx_len, zero)
    o_lat, _ = run(slice(1, None), nseg - 1, seg_len, state)
    o = jnp.concatenate([o_ctx, o_lat], axis=0).reshape(nseg, seg_len, GLA_HEADS, GLA_DV)
    o = _rmsnorm(o, out_g) * jax.nn.silu(g.reshape(nseg, seg_len, GLA_HEADS, GLA_DV))
    mix = jnp.concatenate([o.reshape(nseg, seg_len, GLA_WIDTH), fnet_mix(xf, n_ctx, ctx_len)], axis=-1)
    return mix, s_ctx


def _odd_mix(h, n_ctx, ctx_len, w_in, qk_g, conv_w, a_log, dt_bias, out_g, ctx_k, ctx_v, state):
    nseg, seg_len, _ = h.shape
    proj = matmul(h.reshape(nseg * seg_len, D_MODEL), w_in).reshape(nseg, seg_len, -1)
    qa, ka, va, qkv, z, b_f, b_b, a_f, a_b = _split_cols(proj, ODD_SPLITS)
    beta = jax.nn.sigmoid(jnp.stack([b_f, b_b]))
    logd = jnp.stack([-jnp.exp(a_log[j]) * jax.nn.softplus(a + dt_bias[j]) for j, a in enumerate((a_f, a_b))])

    def run(sl, bsz, t, s0, latent):
        def r(a):
            return a[sl].reshape(bsz, t, a.shape[-1])
        q4 = _rmsnorm(r(qa).reshape(bsz, t, ATT_HEADS, ATT_HD), qk_g[0])
        k4 = _rmsnorm(r(ka).reshape(bsz, t, ATT_KV_HEADS, ATT_HD), qk_g[1])
        v4 = r(va).reshape(bsz, t, ATT_KV_HEADS, ATT_HD)
        if latent:
            q4, k4 = _axial_rope(q4), _axial_rope(k4)
            k_all = jnp.concatenate([ctx_k, k4], axis=1)
            v_all = jnp.concatenate([ctx_v, v4], axis=1)
        else:
            k_all, v_all = k4, v4
        t_k = k_all.shape[1]
        o_att = attention(q4.reshape(bsz, t, ATT_WIDTH).astype(BF16),
                          k_all.reshape(bsz, t_k, KV_WIDTH).astype(BF16),
                          v_all.reshape(bsz, t_k, KV_WIDTH).astype(BF16))
        c = jax.nn.silu(_dwconv(r(qkv), conv_w))
        qd, kd, vd = jnp.split(c, 3, axis=-1)
        qd = _l2norm(qd.reshape(bsz, t, GDN_HEADS, GDN_DK)) * (GDN_DK ** -0.5)
        kd = _l2norm(kd.reshape(bsz, t, GDN_HEADS, GDN_DK))
        o, sfin = gdn_scan(qd.reshape(bsz, t, GDN_WIDTH), kd.reshape(bsz, t, GDN_WIDTH), vd,
                           beta[:, sl].reshape(2, bsz, t, GDN_HEADS), logd[:, sl].reshape(2, bsz, t, GDN_HEADS), s0)
        o_d = (o[0] + o[1]).reshape(-1, seg_len, GDN_WIDTH)
        return o_att.reshape(-1, seg_len, ATT_WIDTH), o_d, k4, v4, sfin

    zero = jnp.zeros((n_ctx, 2, GDN_HEADS, GDN_DK, GDN_DV), F32)
    att_c, od_c, k_ctx, v_ctx, s_ctx = run(slice(0, 1), n_ctx, ctx_len, zero, False)
    att_l, od_l, _, _, _ = run(slice(1, None), nseg - 1, seg_len, state, True)
    o_d = jnp.concatenate([od_c, od_l], axis=0).reshape(nseg, seg_len, GDN_HEADS, GDN_DV)
    o_d = _rmsnorm(o_d, out_g) * jax.nn.silu(z.reshape(nseg, seg_len, GDN_HEADS, GDN_DV))
    mix = jnp.concatenate([jnp.concatenate([att_c, att_l], axis=0), o_d.reshape(nseg, seg_len, GDN_WIDTH)], axis=-1)
    return mix, k_ctx, v_ctx, s_ctx


def kernel(x_prompt, x_sample, cache_k, cache_v, state_gla, state_delta, c, c_ctx, ada_w, ada_b, norm_g, w_out, even_w_in, gla_gate_w2, gla_gate_b, gla_out_norm, odd_w_in, qk_norm, conv_w, gdn_a_log, gdn_dt_bias, gdn_out_norm, peer_wq, peer_keys, peer_u, peer_v):
    n_ctx, ctx_len, d = x_prompt.shape
    n_lat, seg_len, _ = x_sample.shape
    assert n_ctx * ctx_len == seg_len and d == D_MODEL
    depth = ada_w.shape[0]
    nseg = n_lat + 1
    x = jnp.concatenate([x_prompt.reshape(1, seg_len, d), x_sample], axis=0)
    cvec = jnp.concatenate([c_ctx[None, :], c], axis=0)
    new_gla, new_delta, new_k, new_v = [], [], [], []
    for l in range(depth):
        i = l // 2
        mod = matmul(jax.nn.silu(cvec), ada_w[l], tn=1024) + ada_b[l]
        sh1, sc1, g1, sh2, sc2, g2 = [m[:, None, :] for m in jnp.split(mod, 6, axis=-1)]
        h = _rmsnorm(x, norm_g[l, 0]) * (1 + sc1) + sh1
        if l % 2 == 0:
            mix, s_new = _even_mix(h, n_ctx, ctx_len, even_w_in[i], gla_gate_w2[i], gla_gate_b[i], gla_out_norm[i],
                                   state_gla[:, i])
            new_gla.append(s_new)
        else:
            mix, k_new, v_new, s_new = _odd_mix(h, n_ctx, ctx_len, odd_w_in[i], qk_norm[i], conv_w[i], gdn_a_log[i],
                                                gdn_dt_bias[i], gdn_out_norm[i], cache_k[:, i], cache_v[:, i],
                                                state_delta[:, i])
            new_k.append(k_new)
            new_v.append(v_new)
            new_delta.append(s_new)
        y = matmul(mix.reshape(nseg * seg_len, d), w_out[l]).reshape(nseg, seg_len, d)
        x = x + g1 * _rmsnorm(y, norm_g[l, 1])
        h = _rmsnorm(x, norm_g[l, 2]) * (1 + sc2) + sh2
        ht = h.reshape(nseg * seg_len, d).T.astype(BF16)
        pt = peer(ht, peer_wq[l].T.astype(BF16), peer_keys[l], peer_u[l].astype(BF16), peer_v[l].T.astype(BF16))
        x = x + g2 * _rmsnorm(pt.T.reshape(nseg, seg_len, d), norm_g[l, 3])
    return (x[0].reshape(n_ctx, ctx_len, d), x[1:],
            jnp.stack(new_k, axis=1), jnp.stack(new_v, axis=1),
            jnp.stack(new_gla, axis=1), jnp.stack(new_delta, axis=1))
```

```python
import functools

import numpy as np
import jax
import jax.numpy as jnp
from jax import lax
from jax.experimental import pallas as pl
from jax.experimental.pallas import tpu as pltpu

F32 = jnp.float32
BF16 = jnp.bfloat16
HIGHEST = lax.Precision.HIGHEST

D_MODEL = 2048
EPS = 1e-6
CHUNK = 64

GLA_HEADS = 6
GLA_DK = 128
GLA_DV = 256
GLA_WIDTH = GLA_HEADS * GLA_DV
GLA_GATE_RANK = 16
GLA_GATE_NORM = 16.0
FNET_WIDTH = D_MODEL - GLA_WIDTH
FNET_GROUPS = 4
FNET_CH = FNET_WIDTH // FNET_GROUPS

ATT_HD = 128
ATT_HEADS = 8
ATT_KV_HEADS = 2
ATT_GROUP = ATT_HEADS // ATT_KV_HEADS
ATT_WIDTH = ATT_HEADS * ATT_HD
KV_WIDTH = ATT_KV_HEADS * ATT_HD
AXIS_DIM = ATT_HD // 2
ROPE_THETA = 10000.0
GRID_W = 64
GDN_HEADS = 8
GDN_DK = 128
GDN_DV = 128
GDN_WIDTH = GDN_HEADS * GDN_DV
CONV_W = 3

PEER_HEADS = 8
PEER_NKEYS = 128
PEER_HALF = 128
PEER_TOPK = 16
N_EXPERTS = PEER_NKEYS ** 2

EVEN_SPLITS = (GLA_HEADS * GLA_DK, GLA_HEADS * GLA_DK, GLA_WIDTH, GLA_WIDTH, GLA_GATE_RANK, GLA_GATE_RANK, FNET_WIDTH)
ODD_SPLITS = (ATT_WIDTH, KV_WIDTH, KV_WIDTH, 3 * GDN_WIDTH, GDN_WIDTH, GDN_HEADS, GDN_HEADS, GDN_HEADS, GDN_HEADS)

VMEM_LIMIT_BYTES = 56 * 1024 * 1024
NEG_INF = float("-inf")

NT_DIMS = (((1,), (1,)), ((), ()))


def _params(*semantics):
    return pltpu.CompilerParams(dimension_semantics=semantics, vmem_limit_bytes=VMEM_LIMIT_BYTES)


def _split_cols(x, sizes):
    return jnp.split(x, [int(s) for s in np.cumsum(sizes)[:-1]], axis=-1)


def _round_up(n, m):
    return -(-n // m) * m


def _mm_kernel(x_ref, w_ref, o_ref, acc_ref):
    k = pl.program_id(2)

    @pl.when(k == 0)
    def _():
        acc_ref[...] = jnp.zeros_like(acc_ref)

    acc_ref[...] += jnp.dot(x_ref[...].astype(BF16), w_ref[...].astype(BF16), preferred_element_type=F32)

    @pl.when(k == pl.num_programs(2) - 1)
    def _():
        o_ref[...] = acc_ref[...]


def _pick_tile(n, pref):
    for t in (pref, 1024, 512, 256, 128):
        if t <= pref and n % t == 0:
            return t
    return n


def matmul(x, w, *, tm=1024, tn=512, tk=2048):
    m, kdim = x.shape
    n = w.shape[1]
    m_pad = _round_up(m, 8)
    n_pad = _round_up(n, 256)
    if m_pad != m:
        x = jnp.pad(x, ((0, m_pad - m), (0, 0)))
    if n_pad != n:
        w = jnp.pad(w, ((0, 0), (0, n_pad - n)))
    tm = _pick_tile(m_pad, tm)
    tn = _pick_tile(n_pad, tn)
    tk = _pick_tile(kdim, tk)
    out = pl.pallas_call(
        _mm_kernel,
        grid=(m_pad // tm, n_pad // tn, kdim // tk),
        in_specs=[pl.BlockSpec((tm, tk), lambda i, j, k: (i, k)),
                  pl.BlockSpec((tk, tn), lambda i, j, k: (k, j))],
        out_specs=pl.BlockSpec((tm, tn), lambda i, j, k: (i, j)),
        out_shape=jax.ShapeDtypeStruct((m_pad, n_pad), F32),
        scratch_shapes=[pltpu.VMEM((tm, tn), F32)],
        compiler_params=_params("parallel", "parallel", "arbitrary"),
        name="matmul",
    )(x, w)
    return out[:m, :n]


def _rms_rows(x):
    return x * lax.rsqrt(jnp.mean(x * x, axis=-1, keepdims=True) + EPS)


def _norm_matmul_kernel(x_ref, gain_ref, scale_ref, shift_ref, w_ref, *rest, emit_h):
    if emit_h:
        o_ref, h_ref, h_scr = rest
    else:
        o_ref, h_scr = rest

    @pl.when(pl.program_id(2) == 0)
    def _():
        h = _rms_rows(x_ref[0]) * (gain_ref[...] * scale_ref[0]) + shift_ref[0]
        h_scr[...] = h.astype(BF16)
        if emit_h:
            h_ref[0] = h_scr[...]

    o_ref[0] = jnp.dot(h_scr[...], w_ref[...], preferred_element_type=F32)


def norm_matmul(x, gain, scale, shift, w, *, tm, tn, emit_h=False):
    nseg, seg_len, d = x.shape
    n = w.shape[1]
    tm = min(tm, seg_len)
    out_shape = [jax.ShapeDtypeStruct((nseg, seg_len, n), F32)]
    out_specs = [pl.BlockSpec((1, tm, tn), lambda s, i, j: (s, i, j))]
    if emit_h:
        out_shape.append(jax.ShapeDtypeStruct((nseg, seg_len, d), BF16))
        out_specs.append(pl.BlockSpec((1, tm, d), lambda s, i, j: (s, i, 0)))
    res = pl.pallas_call(
        functools.partial(_norm_matmul_kernel, emit_h=emit_h),
        grid=(nseg, seg_len // tm, n // tn),
        in_specs=[pl.BlockSpec((1, tm, d), lambda s, i, j: (s, i, 0)),
                  pl.BlockSpec((1, d), lambda s, i, j: (0, 0)),
                  pl.BlockSpec((1, 1, d), lambda s, i, j: (s, 0, 0)),
                  pl.BlockSpec((1, 1, d), lambda s, i, j: (s, 0, 0)),
                  pl.BlockSpec((d, tn), lambda s, i, j: (0, j))],
        out_specs=out_specs,
        out_shape=out_shape,
        scratch_shapes=[pltpu.VMEM((tm, d), BF16)],
        compiler_params=_params("parallel", "parallel", "arbitrary"),
        name="norm_matmul",
    )(x, gain, scale, shift, w)
    return res if emit_h else res[0]


def _proj_residual_kernel(of_ref, ob_ref, z_ref, hg_ref, b_ref, w_ref, x_ref, gain_ref, gate_ref, o_ref,
                          *, heads, hd, normed_first):
    ka = heads * hd
    o = of_ref[0] + ob_ref[0]
    z = z_ref[0]
    parts = []
    for h in range(heads):
        hs = slice(h * hd, (h + 1) * hd)
        zh = z[:, hs]
        parts.append((_rms_rows(o[:, hs]) * hg_ref[...] * (zh * jax.nn.sigmoid(zh))).astype(BF16))
    a = jnp.concatenate(parts, axis=-1)
    kb = w_ref.shape[0] - ka
    wa, wb = (w_ref[:ka, :], w_ref[ka:, :]) if normed_first else (w_ref[kb:, :], w_ref[:kb, :])
    y = jnp.dot(a, wa, preferred_element_type=F32) + jnp.dot(b_ref[0].astype(BF16), wb, preferred_element_type=F32)
    o_ref[0] = x_ref[0] + gate_ref[0] * (_rms_rows(y) * gain_ref[...])


def proj_residual(o_f, o_b, z, head_gain, b, w, x, gain, gate, *, heads, hd, z_block, normed_first, tm=256):
    nseg, seg_len, d = x.shape
    ka = heads * hd
    kb = d - ka

    def row_spec(width, col=0):
        return pl.BlockSpec((1, tm, width), lambda s, i: (s, i, col))

    return pl.pallas_call(
        functools.partial(_proj_residual_kernel, heads=heads, hd=hd, normed_first=normed_first),
        grid=(nseg, seg_len // tm),
        in_specs=[row_spec(ka), row_spec(ka), row_spec(ka, z_block),
                  pl.BlockSpec((1, hd), lambda s, i: (0, 0)),
                  row_spec(kb),
                  pl.BlockSpec((d, d), lambda s, i: (0, 0)),
                  row_spec(d),
                  pl.BlockSpec((1, d), lambda s, i: (0, 0)),
                  pl.BlockSpec((1, 1, d), lambda s, i: (s, 0, 0))],
        out_specs=row_spec(d),
        out_shape=jax.ShapeDtypeStruct((nseg, seg_len, d), F32),
        compiler_params=_params("parallel", "parallel"),
        name="proj_residual",
    )(o_f, o_b, z, head_gain, b, w, x, gain, gate)


def _split_hi_lo(x):
    hi = x.astype(BF16)
    lo = (x - hi.astype(F32)).astype(BF16)
    return hi, lo


def _gla_kernel(q_ref, k_ref, v_ref, lr_ref, w2_ref, gb_ref, s0_ref, o_ref, sfin_ref, st_scr):
    d = pl.program_id(1)
    c = pl.program_id(2)
    rev = d == 1

    @pl.when(c == 0)
    def _():
        st_scr[...] = s0_ref[0, 0]

    ii = lax.broadcasted_iota(jnp.int32, (CHUNK, CHUNK), 0)
    jj = lax.broadcasted_iota(jnp.int32, (CHUNK, CHUNK), 1)
    incl = (jj - ii) * (1 - 2 * d) <= 0
    tri = jnp.where(incl, 1.0, 0.0).astype(BF16)

    pre = jnp.dot(lr_ref[0, 0].astype(BF16), w2_ref[0].astype(BF16), preferred_element_type=F32) + gb_ref[0]
    logd = (jnp.minimum(pre, 0.0) - jnp.log1p(jnp.exp(-jnp.abs(pre)))) / GLA_GATE_NORM
    g_hi, g_lo = _split_hi_lo(logd)
    b = jnp.dot(tri, g_hi, preferred_element_type=F32) + jnp.dot(tri, g_lo, preferred_element_type=F32)
    mid = CHUNK // 2
    b_ref = jnp.where(rev, b[mid - 1:mid], b[mid:mid + 1])
    b_last = jnp.where(rev, b[0:1], b[CHUNK - 1:CHUNK])
    e_q = jnp.exp(b - b_ref)
    e_k = jnp.exp(b_ref - b)
    e_b = jnp.exp(b)
    e_kl = jnp.exp(b_last - b)
    e_l = jnp.exp(b_last)

    for h in range(GLA_HEADS):
        ks = slice(h * GLA_DK, (h + 1) * GLA_DK)
        vs = slice(h * GLA_DV, (h + 1) * GLA_DV)
        qi = q_ref[0, :, ks] * (GLA_DK ** -0.5)
        ki = k_ref[0, :, ks]
        vi = v_ref[0, :, vs].astype(BF16)
        a = lax.dot_general((qi * e_q[:, ks]).astype(BF16), (ki * e_k[:, ks]).astype(BF16), NT_DIMS,
                            preferred_element_type=F32)
        a = jnp.where(incl, a, 0.0)
        st = st_scr[h]
        o = lax.dot_general((qi * e_b[:, ks]).astype(BF16), st.astype(BF16), NT_DIMS,
                            preferred_element_type=F32)
        o = o + jnp.dot(a.astype(BF16), vi, preferred_element_type=F32)
        o_ref[0, 0, :, vs] = o
        kd = (ki * e_kl[:, ks]).astype(BF16)
        st_scr[h] = st * e_l[:, ks] + jnp.dot(vi.T, kd, preferred_element_type=F32)

    @pl.when(c == pl.num_programs(2) - 1)
    def _():
        sfin_ref[0, 0] = st_scr[...]


def gla_scan(q, k, v, lr, w2, gb, s0):
    bsz, t, _ = q.shape
    n = t // CHUNK
    s0t = jnp.swapaxes(s0, -1, -2)

    def tmap(b, d, c):
        return (b, c + d * (n - 1 - 2 * c), 0)

    o, sfin = pl.pallas_call(
        _gla_kernel,
        grid=(bsz, 2, n),
        in_specs=[pl.BlockSpec((1, CHUNK, GLA_HEADS * GLA_DK), tmap),
                  pl.BlockSpec((1, CHUNK, GLA_HEADS * GLA_DK), tmap),
                  pl.BlockSpec((1, CHUNK, GLA_WIDTH), tmap),
                  pl.BlockSpec((1, 1, CHUNK, GLA_GATE_RANK), lambda b, d, c: (d, b, c + d * (n - 1 - 2 * c), 0)),
                  pl.BlockSpec((1, GLA_GATE_RANK, GLA_HEADS * GLA_DK), lambda b, d, c: (d, 0, 0)),
                  pl.BlockSpec((1, 1, GLA_HEADS * GLA_DK), lambda b, d, c: (d, 0, 0)),
                  pl.BlockSpec((1, 1, GLA_HEADS, GLA_DV, GLA_DK), lambda b, d, c: (b, d, 0, 0, 0))],
        out_specs=[pl.BlockSpec((1, 1, CHUNK, GLA_WIDTH), lambda b, d, c: (d, b, c + d * (n - 1 - 2 * c), 0)),
                   pl.BlockSpec((1, 1, GLA_HEADS, GLA_DV, GLA_DK), lambda b, d, c: (b, d, 0, 0, 0))],
        out_shape=[jax.ShapeDtypeStruct((2, bsz, t, GLA_WIDTH), F32),
                   jax.ShapeDtypeStruct((bsz, 2, GLA_HEADS, GLA_DV, GLA_DK), F32)],
        scratch_shapes=[pltpu.VMEM((GLA_HEADS, GLA_DV, GLA_DK), F32)],
        compiler_params=_params("parallel", "arbitrary", "arbitrary"),
        name="gla_scan",
    )(q, k, v, lr, w2, gb, s0t)
    return o, jnp.swapaxes(sfin, -1, -2)


def _gdn_kernel(qf_ref, kf_ref, vf_ref, qb_ref, kb_ref, vb_ref, beta_f_ref, g_f_ref, gt_f_ref,
                beta_b_ref, g_b_ref, gt_b_ref, s0_ref, of_ref, ob_ref, sfin_ref, s_scr):
    c = pl.program_id(1)

    @pl.when(c == 0)
    def _():
        s_scr[...] = s0_ref[0]

    ii = lax.broadcasted_iota(jnp.int32, (CHUNK, CHUNK), 0)
    jj = lax.broadcasted_iota(jnp.int32, (CHUNK, CHUNK), 1)
    eye = jnp.where(ii == jj, 1.0, 0.0).astype(F32)

    jobs = []
    for d, (q_ref, k_ref, v_ref, beta_ref, g_ref, gt_ref, o_ref) in enumerate((
            (qf_ref, kf_ref, vf_ref, beta_f_ref, g_f_ref, gt_f_ref, of_ref),
            (qb_ref, kb_ref, vb_ref, beta_b_ref, g_b_ref, gt_b_ref, ob_ref))):
        order = ii - jj if d else jj - ii
        incl = order <= 0
        strict = order < 0
        tri = jnp.where(incl, 1.0, 0.0).astype(BF16)
        g_hi, g_lo = _split_hi_lo(g_ref[0, 0])
        dcol = jnp.dot(tri, g_hi, preferred_element_type=F32) + jnp.dot(tri, g_lo, preferred_element_type=F32)
        gt_hi, gt_lo = _split_hi_lo(gt_ref[0, 0, 0])
        drow = (lax.dot_general(gt_hi, tri, NT_DIMS, preferred_element_type=F32)
                + lax.dot_general(gt_lo, tri, NT_DIMS, preferred_element_type=F32))
        beta = beta_ref[0, 0]
        last = 0 if d else CHUNK - 1
        for h in range(GDN_HEADS):
            hs = slice(h * GDN_DK, (h + 1) * GDN_DK)
            qi = q_ref[0, :, hs]
            ki = k_ref[0, :, hs]
            dc = dcol[:, h:h + 1]
            bc = beta[:, h:h + 1]
            decay = jnp.exp(jnp.where(incl, dc - drow[h:h + 1, :], NEG_INF))
            kb = ki * bc
            kq = jnp.concatenate([kb, qi], axis=0).astype(BF16)
            kqk = lax.dot_general(kq, ki.astype(BF16), NT_DIMS, preferred_element_type=F32)
            a = jnp.where(strict, kqk[:CHUNK] * decay, 0.0)
            e_d = jnp.exp(dc)
            d_last = dc[last:last + 1]
            jobs.append(dict(
                d=d, h=h, hs=hs, o_ref=o_ref, p=a, tinv=eye - a, att=(kqk[CHUNK:] * decay).astype(BF16),
                rhs=jnp.concatenate([v_ref[0, :, hs] * bc, kb * e_d], axis=-1).astype(BF16),
                qe=qi * e_d, kd=(ki * jnp.exp(d_last - dc)).astype(BF16), s_decay=jnp.exp(d_last)))

    for _ in range(5):
        for j in jobs:
            pb = j["p"].astype(BF16)
            j["p"] = jnp.dot(pb, pb, preferred_element_type=F32)
        for j in jobs:
            j["tinv"] = j["tinv"] + jnp.dot(j["tinv"].astype(BF16), j["p"].astype(BF16), preferred_element_type=F32)
    for j in jobs:
        j["sol"] = jnp.dot(j["tinv"].astype(BF16), j["rhs"], preferred_element_type=F32)
    for j in jobs:
        j["s"] = s_scr[j["d"], j["h"]]
        wq = jnp.concatenate([j["sol"][:, GDN_DV:], j["qe"]], axis=0).astype(BF16)
        j["ws"] = jnp.dot(wq, j["s"].astype(BF16), preferred_element_type=F32)
    for j in jobs:
        j["vnb"] = (j["sol"][:, :GDN_DV] - j["ws"][:CHUNK]).astype(BF16)
        j["o"] = j["ws"][CHUNK:] + jnp.dot(j["att"], j["vnb"], preferred_element_type=F32)
    for j in jobs:
        j["s_new"] = j["s_decay"] * j["s"] + jnp.dot(j["kd"].T, j["vnb"], preferred_element_type=F32)
    for j in jobs:
        j["o_ref"][0, :, j["hs"]] = j["o"]
        s_scr[j["d"], j["h"]] = j["s_new"]

    @pl.when(c == pl.num_programs(1) - 1)
    def _():
        sfin_ref[0] = s_scr[...]


def gdn_scan(q, k, v, beta, logd, s0):
    bsz, t, _ = q.shape
    n = t // CHUNK
    logd_t = jnp.swapaxes(logd.reshape(2, bsz, n, CHUNK, GDN_HEADS), -1, -2)

    def seq_spec(rev):
        return pl.BlockSpec((1, CHUNK, GDN_WIDTH), lambda b, c: (b, n - 1 - c if rev else c, 0))

    def gate_specs(d):
        cm = (lambda c: n - 1 - c) if d else (lambda c: c)
        return [pl.BlockSpec((1, 1, CHUNK, GDN_HEADS), lambda b, c: (d, b, cm(c), 0)),
                pl.BlockSpec((1, 1, CHUNK, GDN_HEADS), lambda b, c: (d, b, cm(c), 0)),
                pl.BlockSpec((1, 1, 1, GDN_HEADS, CHUNK), lambda b, c: (d, b, cm(c), 0, 0))]

    state_spec = pl.BlockSpec((1, 2, GDN_HEADS, GDN_DK, GDN_DV), lambda b, c: (b, 0, 0, 0, 0))
    o_f, o_b, sfin = pl.pallas_call(
        _gdn_kernel,
        grid=(bsz, n),
        in_specs=[seq_spec(False)] * 3 + [seq_spec(True)] * 3 + gate_specs(0) + gate_specs(1) + [state_spec],
        out_specs=[seq_spec(False), seq_spec(True), state_spec],
        out_shape=[jax.ShapeDtypeStruct((bsz, t, GDN_WIDTH), F32),
                   jax.ShapeDtypeStruct((bsz, t, GDN_WIDTH), F32),
                   jax.ShapeDtypeStruct((bsz, 2, GDN_HEADS, GDN_DK, GDN_DV), F32)],
        scratch_shapes=[pltpu.VMEM((2, GDN_HEADS, GDN_DK, GDN_DV), F32)],
        compiler_params=_params("parallel", "arbitrary"),
        name="gdn_scan",
    )(q, k, v, q, k, v, beta, logd, logd_t, beta, logd, logd_t, s0)
    return o_f, o_b, sfin


def _attn_kernel(q_ref, k_ref, v_ref, o_ref):
    k = k_ref[0]
    v = v_ref[0]
    for g in range(ATT_GROUP):
        gs = slice(g * ATT_HD, (g + 1) * ATT_HD)
        s = lax.dot_general(q_ref[0, :, gs], k, NT_DIMS, preferred_element_type=F32) * (ATT_HD ** -0.5)
        m = jnp.max(s, axis=-1, keepdims=True)
        p = jnp.exp(s - m)
        l = jnp.sum(p, axis=-1, keepdims=True)
        o = jnp.dot(p.astype(BF16), v, preferred_element_type=F32)
        o_ref[0, :, gs] = o / l


def attention(q, k, v, *, tq=256):
    bsz, t_q, _ = q.shape
    t_k = k.shape[1]
    tq = min(tq, t_q)
    gw = ATT_GROUP * ATT_HD
    return pl.pallas_call(
        _attn_kernel,
        grid=(bsz, ATT_KV_HEADS, t_q // tq),
        in_specs=[pl.BlockSpec((1, tq, gw), lambda b, h, i: (b, i, h)),
                  pl.BlockSpec((1, t_k, ATT_HD), lambda b, h, i: (b, 0, h)),
                  pl.BlockSpec((1, t_k, ATT_HD), lambda b, h, i: (b, 0, h))],
        out_specs=pl.BlockSpec((1, tq, gw), lambda b, h, i: (b, i, h)),
        out_shape=jax.ShapeDtypeStruct((bsz, t_q, ATT_WIDTH), F32),
        compiler_params=_params("parallel", "parallel", "parallel"),
        name="attention",
    )(q, k, v)


PEER_TB = 512
PEER_ET = 1024
PEER_IT = PEER_ET // PEER_NKEYS
assert PEER_IT % 8 == 0
PEER_PAIRS = tuple((a, b) for a in range(PEER_TOPK) for b in range(PEER_TOPK) if (a + 1) * (b + 1) <= PEER_TOPK)


def _top_values(x, count):
    vals = []
    for _ in range(count):
        m = jnp.max(x, axis=0, keepdims=True)
        vals.append(m)
        x = jnp.where(x == m, NEG_INF, x)
    return vals


PEER_CAND_ROWS = _round_up(len(PEER_PAIRS), 8)


def _peer_route_kernel(q_ref, keys_ref, p1_ref, e2_ref, tau_ref, cand_scr):
    def top_candidates(rows):
        cand_scr[PEER_CAND_ROWS - 8:, :] = jnp.full((8, cand_scr.shape[1]), NEG_INF, F32)
        for r, row in enumerate(rows):
            cand_scr[r:r + 1, :] = row
        return _top_values(cand_scr[...], PEER_TOPK)

    for h in range(PEER_HEADS):
        e = []
        tops = []
        for p in range(2):
            r0 = (h * 2 + p) * PEER_HALF
            s = lax.dot_general(keys_ref[h, p], q_ref[:, r0:r0 + PEER_HALF], NT_DIMS, precision=HIGHEST,
                                preferred_element_type=F32)
            top = _top_values(s, PEER_TOPK)
            e.append(jnp.exp(s - top[0]))
            tops.append([jnp.exp(t - top[0]) for t in top])
        z = sum(top_candidates([tops[0][a] * tops[1][b] for a, b in PEER_PAIRS]))
        zinv = 1.0 / z
        p1 = e[0] * zinv
        tau = top_candidates([(tops[0][a] * zinv) * tops[1][b] for a, b in PEER_PAIRS])[-1]
        p1_ref[h] = p1
        e2_ref[h] = e[1]
        tau_ref[h] = jnp.broadcast_to(tau, (8, tau.shape[1]))


def _gelu_tanh(x):
    return 0.5 * x * (1.0 + jnp.tanh(np.sqrt(2.0 / np.pi).astype(np.float32) * (x + 0.044715 * (x * x * x))))


def _peer_dense_kernel(h_ref, u_ref, vt_ref, p1_ref, e2_ref, tau_ref, o_ref, coef_scr):
    @pl.when(pl.program_id(1) == 0)
    def _():
        o_ref[...] = jnp.zeros_like(o_ref)

    act = _gelu_tanh(lax.dot_general(u_ref[...], h_ref[...], NT_DIMS, preferred_element_type=F32))
    tb = h_ref.shape[0]
    for ts in range(tb // 128):
        ls = slice(ts * 128, (ts + 1) * 128)
        for i in range(PEER_IT):
            gate = jnp.zeros((PEER_NKEYS, 128), F32)
            for h in range(PEER_HEADS):
                w = e2_ref[h, :, ls] * p1_ref[h, i:i + 1, ls]
                gate = gate + jnp.where(w >= tau_ref[h, 0:1, ls], w, 0.0)
            rs = slice(i * PEER_NKEYS, (i + 1) * PEER_NKEYS)
            coef_scr[rs, ls] = (gate * act[rs, ls]).astype(BF16)
    o_ref[...] += jnp.dot(vt_ref[...], coef_scr[...], preferred_element_type=F32)


def _peer_residual_kernel(x_ref, yt_ref, gain_ref, gate_ref, o_ref):
    o_ref[...] = x_ref[...] + gate_ref[0] * (_rms_rows(yt_ref[...].T) * gain_ref[...])


def peer_residual(x, q, h, keys, u, v_t, gain, gate):
    nseg, seg_len, d = x.shape
    n = nseg * seg_len
    nb = n // PEER_TB
    p1, e2, tau = pl.pallas_call(
        _peer_route_kernel,
        grid=(nb,),
        in_specs=[pl.BlockSpec((PEER_TB, PEER_HEADS * 2 * PEER_HALF), lambda i: (i, 0)),
                  pl.BlockSpec((PEER_HEADS, 2, PEER_NKEYS, PEER_HALF), lambda i: (0, 0, 0, 0))],
        out_specs=[pl.BlockSpec((PEER_HEADS, PEER_NKEYS, PEER_TB), lambda i: (0, 0, i)),
                   pl.BlockSpec((PEER_HEADS, PEER_NKEYS, PEER_TB), lambda i: (0, 0, i)),
                   pl.BlockSpec((PEER_HEADS, 8, PEER_TB), lambda i: (0, 0, i))],
        out_shape=[jax.ShapeDtypeStruct((PEER_HEADS, PEER_NKEYS, n), F32),
                   jax.ShapeDtypeStruct((PEER_HEADS, PEER_NKEYS, n), F32),
                   jax.ShapeDtypeStruct((PEER_HEADS, 8, n), F32)],
        scratch_shapes=[pltpu.VMEM((PEER_CAND_ROWS, PEER_TB), F32)],
        compiler_params=_params("parallel"),
        name="peer_route",
    )(q.reshape(n, -1), keys)
    n_tiles = N_EXPERTS // PEER_ET
    y_t = pl.pallas_call(
        _peer_dense_kernel,
        grid=(nb, n_tiles),
        in_specs=[pl.BlockSpec((PEER_TB, d), lambda i, e: (i, 0)),
                  pl.BlockSpec((PEER_ET, d), lambda i, e: (e, 0)),
                  pl.BlockSpec((d, PEER_ET), lambda i, e: (0, e)),
                  pl.BlockSpec((PEER_HEADS, PEER_IT, PEER_TB), lambda i, e: (0, e, i)),
                  pl.BlockSpec((PEER_HEADS, PEER_NKEYS, PEER_TB), lambda i, e: (0, 0, i)),
                  pl.BlockSpec((PEER_HEADS, 8, PEER_TB), lambda i, e: (0, 0, i))],
        out_specs=pl.BlockSpec((d, PEER_TB), lambda i, e: (0, i)),
        out_shape=jax.ShapeDtypeStruct((d, n), F32),
        scratch_shapes=[pltpu.VMEM((PEER_ET, PEER_TB), BF16)],
        compiler_params=_params("parallel", "arbitrary"),
        name="peer_dense",
    )(h.reshape(n, d), u, v_t, p1, e2, tau)
    tr = 256
    per_seg = seg_len // tr
    return pl.pallas_call(
        _peer_residual_kernel,
        grid=(n // tr,),
        in_specs=[pl.BlockSpec((tr, d), lambda i: (i, 0)),
                  pl.BlockSpec((d, tr), lambda i: (0, i)),
                  pl.BlockSpec((1, d), lambda i: (0, 0)),
                  pl.BlockSpec((1, 1, d), lambda i: (i // per_seg, 0, 0))],
        out_specs=pl.BlockSpec((tr, d), lambda i: (i, 0)),
        out_shape=jax.ShapeDtypeStruct((n, d), F32),
        compiler_params=_params("parallel"),
        name="peer_residual",
    )(x.reshape(n, d), y_t, gain, gate).reshape(nseg, seg_len, d)


def _rmsnorm(x, g):
    return x * lax.rsqrt(jnp.mean(x * x, axis=-1, keepdims=True) + EPS) * g


def _l2norm(x):
    return x * lax.rsqrt(jnp.sum(x * x, axis=-1, keepdims=True) + EPS)


def _dft_tables(t):
    idx = jnp.arange(t, dtype=jnp.int32)
    ang = ((idx[:, None] * idx[None, :]) % t).astype(F32) * (2.0 * np.pi / t)
    return jnp.cos(ang), jnp.sin(ang)


def fnet_mix(xf, n_ctx_seq, ctx_len):
    nseg, seg_len, _ = xf.shape
    cc, sc = _dft_tables(FNET_CH)
    eye = jnp.eye(FNET_GROUPS, dtype=F32)
    w_ch = jnp.concatenate([jnp.kron(eye, cc), jnp.kron(eye, sc)], axis=1)
    z = matmul(xf.reshape(nseg * seg_len, FNET_WIDTH), w_ch).reshape(nseg, seg_len, 2 * FNET_WIDTH)

    def seq_dft(zz, t):
        bsz = zz.shape[0]
        ct, st = _dft_tables(t)
        scale = (t * FNET_CH) ** -0.5
        w_t = jnp.concatenate([ct, -st], axis=1) * scale
        stacked = jnp.concatenate([zz[..., :FNET_WIDTH], zz[..., FNET_WIDTH:]], axis=1)
        rhs = jnp.swapaxes(stacked, 0, 1).reshape(2 * t, bsz * FNET_WIDTH)
        y = matmul(w_t.astype(BF16), rhs.astype(BF16), tm=1024, tn=1024)
        return jnp.swapaxes(y.reshape(t, bsz, FNET_WIDTH), 0, 1)

    y_ctx = seq_dft(z[0].reshape(n_ctx_seq, ctx_len, 2 * FNET_WIDTH), ctx_len).reshape(1, seg_len, FNET_WIDTH)
    y_lat = seq_dft(z[1:], seg_len)
    return jnp.concatenate([y_ctx, y_lat], axis=0)


def _axial_rope(x):
    t = x.shape[1]
    rows = t // GRID_W
    row = jnp.repeat(jnp.arange(rows, dtype=F32), GRID_W)
    col = jnp.tile(jnp.arange(GRID_W, dtype=F32), rows)
    freqs = ROPE_THETA ** (-jnp.arange(AXIS_DIM // 2, dtype=F32) * 2.0 / AXIS_DIM)

    def rot(xa, pos):
        ang = pos[:, None] * freqs[None, :]
        cos, sin = jnp.cos(ang)[None, :, None, :], jnp.sin(ang)[None, :, None, :]
        x1, x2 = xa[..., :AXIS_DIM // 2], xa[..., AXIS_DIM // 2:]
        return jnp.concatenate([x1 * cos - x2 * sin, x2 * cos + x1 * sin], axis=-1)

    return jnp.concatenate([rot(x[..., :AXIS_DIM], row), rot(x[..., AXIS_DIM:], col)], axis=-1)


def _dwconv(x, w):
    xp = jnp.pad(x, ((0, 0), (1, 1), (0, 0)))
    return xp[:, :-2] * w[0] + xp[:, 1:-1] * w[1] + xp[:, 2:] * w[2]


EVEN_COLS = (GLA_HEADS * GLA_DK, GLA_HEADS * GLA_DK, GLA_WIDTH, GLA_WIDTH, FNET_WIDTH, GLA_GATE_RANK, GLA_GATE_RANK)
ODD_COLS = (ATT_WIDTH, GDN_WIDTH, GDN_WIDTH, GDN_WIDTH, GDN_WIDTH, KV_WIDTH, KV_WIDTH,
            GDN_HEADS, GDN_HEADS, GDN_HEADS, GDN_HEADS)
PROJ_TN = 256


def _pad_cols(w):
    return jnp.pad(w, ((0, 0), (0, _round_up(w.shape[1], PROJ_TN) - w.shape[1])))


def _even_weight(w_in):
    q, k, v, g, lr_f, lr_b, xf = _split_cols(w_in, EVEN_SPLITS)
    return _pad_cols(jnp.concatenate([q, k, v, g, xf, lr_f, lr_b], axis=1)).astype(BF16)


def _odd_weight(w_in):
    qa, ka, va, qkv, z, b_f, b_b, a_f, a_b = _split_cols(w_in, ODD_SPLITS)
    return _pad_cols(jnp.concatenate([qa, qkv, z, ka, va, b_f, b_b, a_f, a_b], axis=1)).astype(BF16)


def _even_mix(proj, n_ctx, ctx_len, gate_w2, gate_b, state):
    nseg, seg_len, _ = proj.shape
    q, k, v, _, xf, lr_f, lr_b = _split_cols(proj[..., :sum(EVEN_COLS)], EVEN_COLS)
    lr = jnp.stack([lr_f, lr_b])
    gb = gate_b[:, None, :]

    def run(sl, bsz, t, s0):
        def r(a):
            return a[sl].reshape(bsz, t, a.shape[-1])
        o, sfin = gla_scan(r(q), r(k), r(v), lr[:, sl].reshape(2, bsz, t, GLA_GATE_RANK), gate_w2, gb, s0)
        return o.reshape(2, -1, seg_len, GLA_WIDTH), sfin

    zero = jnp.zeros((n_ctx, 2, GLA_HEADS, GLA_DK, GLA_DV), F32)
    o_ctx, s_ctx = run(slice(0, 1), n_ctx, ctx_len, zero)
    o_lat, _ = run(slice(1, None), nseg - 1, seg_len, state)
    o = jnp.concatenate([o_ctx, o_lat], axis=1)
    return o[0], o[1], fnet_mix(xf, n_ctx, ctx_len), s_ctx


def _odd_mix(proj, n_ctx, ctx_len, qk_g, conv_w, a_log, dt_bias, ctx_k, ctx_v, state):
    nseg, seg_len, _ = proj.shape
    qa, qd, kd, vd, _, ka, va, b_f, b_b, a_f, a_b = _split_cols(proj[..., :sum(ODD_COLS)], ODD_COLS)
    qkv = jnp.concatenate([qd, kd, vd], axis=-1)
    beta = jax.nn.sigmoid(jnp.stack([b_f, b_b]))
    logd = jnp.stack([-jnp.exp(a_log[j]) * jax.nn.softplus(a + dt_bias[j]) for j, a in enumerate((a_f, a_b))])

    def run(sl, bsz, t, s0, latent):
        def r(a):
            return a[sl].reshape(bsz, t, a.shape[-1])
        q4 = _rmsnorm(r(qa).reshape(bsz, t, ATT_HEADS, ATT_HD), qk_g[0])
        k4 = _rmsnorm(r(ka).reshape(bsz, t, ATT_KV_HEADS, ATT_HD), qk_g[1])
        v4 = r(va).reshape(bsz, t, ATT_KV_HEADS, ATT_HD)
        if latent:
            q4, k4 = _axial_rope(q4), _axial_rope(k4)
            k_all = jnp.concatenate([ctx_k, k4], axis=1)
            v_all = jnp.concatenate([ctx_v, v4], axis=1)
        else:
            k_all, v_all = k4, v4
        t_k = k_all.shape[1]
        o_att = attention(q4.reshape(bsz, t, ATT_WIDTH).astype(BF16),
                          k_all.reshape(bsz, t_k, KV_WIDTH).astype(BF16),
                          v_all.reshape(bsz, t_k, KV_WIDTH).astype(BF16))
        c = jax.nn.silu(_dwconv(r(qkv), conv_w))
        qc, kc, vc = jnp.split(c, 3, axis=-1)
        qc = _l2norm(qc.reshape(bsz, t, GDN_HEADS, GDN_DK)) * (GDN_DK ** -0.5)
        kc = _l2norm(kc.reshape(bsz, t, GDN_HEADS, GDN_DK))
        o_f, o_b, sfin = gdn_scan(qc.reshape(bsz, t, GDN_WIDTH), kc.reshape(bsz, t, GDN_WIDTH), vc,
                                  beta[:, sl].reshape(2, bsz, t, GDN_HEADS), logd[:, sl].reshape(2, bsz, t, GDN_HEADS), s0)
        return (o_att.reshape(-1, seg_len, ATT_WIDTH), o_f.reshape(-1, seg_len, GDN_WIDTH),
                o_b.reshape(-1, seg_len, GDN_WIDTH), k4, v4, sfin)

    zero = jnp.zeros((n_ctx, 2, GDN_HEADS, GDN_DK, GDN_DV), F32)
    att_c, of_c, ob_c, k_ctx, v_ctx, s_ctx = run(slice(0, 1), n_ctx, ctx_len, zero, False)
    att_l, of_l, ob_l, _, _, _ = run(slice(1, None), nseg - 1, seg_len, state, True)

    def cat(a, b):
        return jnp.concatenate([a, b], axis=0)
    return cat(of_c, of_l), cat(ob_c, ob_l), cat(att_c, att_l), k_ctx, v_ctx, s_ctx


def kernel(x_prompt, x_sample, cache_k, cache_v, state_gla, state_delta, c, c_ctx, ada_w, ada_b, norm_g, w_out, even_w_in, gla_gate_w2, gla_gate_b, gla_out_norm, odd_w_in, qk_norm, conv_w, gdn_a_log, gdn_dt_bias, gdn_out_norm, peer_wq, peer_keys, peer_u, peer_v):
    n_ctx, ctx_len, d = x_prompt.shape
    n_lat, seg_len, _ = x_sample.shape
    assert n_ctx * ctx_len == seg_len and d == D_MODEL
    depth = ada_w.shape[0]
    x = jnp.concatenate([x_prompt.reshape(1, seg_len, d), x_sample], axis=0)
    cvec = jnp.concatenate([c_ctx[None, :], c], axis=0)
    new_gla, new_delta, new_k, new_v = [], [], [], []
    for l in range(depth):
        i = l // 2
        mod = matmul(jax.nn.silu(cvec), ada_w[l], tn=1024) + ada_b[l]
        sh1, sc1, g1, sh2, sc2, g2 = [m[:, None, :] for m in jnp.split(mod, 6, axis=-1)]
        gains = norm_g[l][:, None, :]
        w_o = w_out[l].astype(BF16)
        if l % 2 == 0:
            proj = norm_matmul(x, gains[0], 1 + sc1, sh1, _even_weight(even_w_in[i]), tm=1024, tn=PROJ_TN)
            o_f, o_b, other, s_new = _even_mix(proj, n_ctx, ctx_len, gla_gate_w2[i], gla_gate_b[i], state_gla[:, i])
            new_gla.append(s_new)
            x = proj_residual(o_f, o_b, proj, gla_out_norm[i][None, :], other, w_o, x, gains[1], g1,
                              heads=GLA_HEADS, hd=GLA_DV, z_block=2, normed_first=True)
        else:
            proj = norm_matmul(x, gains[0], 1 + sc1, sh1, _odd_weight(odd_w_in[i]), tm=1024, tn=PROJ_TN)
            o_f, o_b, other, k_new, v_new, s_new = _odd_mix(proj, n_ctx, ctx_len, qk_norm[i], conv_w[i], gdn_a_log[i],
                                                            gdn_dt_bias[i], cache_k[:, i], cache_v[:, i],
                                                            state_delta[:, i])
            new_k.append(k_new)
            new_v.append(v_new)
            new_delta.append(s_new)
            x = proj_residual(o_f, o_b, proj, gdn_out_norm[i][None, :], other, w_o, x, gains[1], g1,
                              heads=GDN_HEADS, hd=GDN_DV, z_block=4, normed_first=False)
        q, h = norm_matmul(x, gains[2], 1 + sc2, sh2, peer_wq[l].astype(BF16), tm=512, tn=512, emit_h=True)
        x = peer_residual(x, q, h, peer_keys[l], peer_u[l].astype(BF16), peer_v[l].T.astype(BF16), gains[3], g2)
    return (x[0].reshape(n_ctx, ctx_len, d), x[1:],
            jnp.stack(new_k, axis=1), jnp.stack(new_v, axis=1),
            jnp.stack(new_gla, axis=1), jnp.stack(new_delta, axis=1))
```

```python
import functools

import numpy as np
import jax
import jax.numpy as jnp
from jax import lax
from jax.experimental import pallas as pl
from jax.experimental.pallas import tpu as pltpu

F32 = jnp.float32
BF16 = jnp.bfloat16
HIGHEST = lax.Precision.HIGHEST

D_MODEL = 2048
EPS = 1e-6
CHUNK = 64

GLA_HEADS = 6
GLA_DK = 128
GLA_DV = 256
GLA_WIDTH = GLA_HEADS * GLA_DV
GLA_GATE_RANK = 16
GLA_GATE_NORM = 16.0
FNET_WIDTH = D_MODEL - GLA_WIDTH
FNET_GROUPS = 4
FNET_CH = FNET_WIDTH // FNET_GROUPS

ATT_HD = 128
ATT_HEADS = 8
ATT_KV_HEADS = 2
ATT_GROUP = ATT_HEADS // ATT_KV_HEADS
ATT_WIDTH = ATT_HEADS * ATT_HD
KV_WIDTH = ATT_KV_HEADS * ATT_HD
AXIS_DIM = ATT_HD // 2
ROPE_THETA = 10000.0
GRID_W = 64
GDN_HEADS = 8
GDN_DK = 128
GDN_DV = 128
GDN_WIDTH = GDN_HEADS * GDN_DV
CONV_W = 3

PEER_HEADS = 8
PEER_NKEYS = 128
PEER_HALF = 128
PEER_TOPK = 16
N_EXPERTS = PEER_NKEYS ** 2

EVEN_SPLITS = (GLA_HEADS * GLA_DK, GLA_HEADS * GLA_DK, GLA_WIDTH, GLA_WIDTH, GLA_GATE_RANK, GLA_GATE_RANK, FNET_WIDTH)
ODD_SPLITS = (ATT_WIDTH, KV_WIDTH, KV_WIDTH, 3 * GDN_WIDTH, GDN_WIDTH, GDN_HEADS, GDN_HEADS, GDN_HEADS, GDN_HEADS)

VMEM_LIMIT_BYTES = 56 * 1024 * 1024
NEG_INF = float("-inf")

NT_DIMS = (((1,), (1,)), ((), ()))


def _params(*semantics):
    return pltpu.CompilerParams(dimension_semantics=semantics, vmem_limit_bytes=VMEM_LIMIT_BYTES)


def _split_cols(x, sizes):
    return jnp.split(x, [int(s) for s in np.cumsum(sizes)[:-1]], axis=-1)


def _round_up(n, m):
    return -(-n // m) * m


def _mm_kernel(x_ref, w_ref, o_ref, acc_ref):
    k = pl.program_id(2)

    @pl.when(k == 0)
    def _():
        acc_ref[...] = jnp.zeros_like(acc_ref)

    acc_ref[...] += jnp.dot(x_ref[...].astype(BF16), w_ref[...].astype(BF16), preferred_element_type=F32)

    @pl.when(k == pl.num_programs(2) - 1)
    def _():
        o_ref[...] = acc_ref[...]


def _pick_tile(n, pref):
    for t in (pref, 1024, 512, 256, 128):
        if t <= pref and n % t == 0:
            return t
    return n


def matmul(x, w, *, tm=1024, tn=512, tk=2048):
    m, kdim = x.shape
    n = w.shape[1]
    m_pad = _round_up(m, 8)
    n_pad = _round_up(n, 256)
    if m_pad != m:
        x = jnp.pad(x, ((0, m_pad - m), (0, 0)))
    if n_pad != n:
        w = jnp.pad(w, ((0, 0), (0, n_pad - n)))
    tm = _pick_tile(m_pad, tm)
    tn = _pick_tile(n_pad, tn)
    tk = _pick_tile(kdim, tk)
    out = pl.pallas_call(
        _mm_kernel,
        grid=(m_pad // tm, n_pad // tn, kdim // tk),
        in_specs=[pl.BlockSpec((tm, tk), lambda i, j, k: (i, k)),
                  pl.BlockSpec((tk, tn), lambda i, j, k: (k, j))],
        out_specs=pl.BlockSpec((tm, tn), lambda i, j, k: (i, j)),
        out_shape=jax.ShapeDtypeStruct((m_pad, n_pad), F32),
        scratch_shapes=[pltpu.VMEM((tm, tn), F32)],
        compiler_params=_params("parallel", "parallel", "arbitrary"),
        name="matmul",
    )(x, w)
    return out[:m, :n]


def _rms_rows(x):
    return x * lax.rsqrt(jnp.mean(x * x, axis=-1, keepdims=True) + EPS)


def _l2_rows(x):
    return x * lax.rsqrt(jnp.sum(x * x, axis=-1, keepdims=True) + EPS)


def _norm_matmul_kernel(x_ref, gain_ref, scale_ref, shift_ref, w_ref, *rest, emit_h):
    if emit_h:
        o_ref, h_ref, h_scr = rest
    else:
        o_ref, h_scr = rest

    @pl.when(pl.program_id(2) == 0)
    def _():
        h = _rms_rows(x_ref[0]) * (gain_ref[...] * scale_ref[0]) + shift_ref[0]
        h_scr[...] = h.astype(BF16)
        if emit_h:
            h_ref[0] = h_scr[...]

    o_ref[0] = jnp.dot(h_scr[...], w_ref[...], preferred_element_type=F32)


def norm_matmul(x, gain, scale, shift, w, *, tm, tn, emit_h=False):
    nseg, seg_len, d = x.shape
    n = w.shape[1]
    tm = min(tm, seg_len)
    out_shape = [jax.ShapeDtypeStruct((nseg, seg_len, n), F32)]
    out_specs = [pl.BlockSpec((1, tm, tn), lambda s, i, j: (s, i, j))]
    if emit_h:
        out_shape.append(jax.ShapeDtypeStruct((nseg, seg_len, d), BF16))
        out_specs.append(pl.BlockSpec((1, tm, d), lambda s, i, j: (s, i, 0)))
    res = pl.pallas_call(
        functools.partial(_norm_matmul_kernel, emit_h=emit_h),
        grid=(nseg, seg_len // tm, n // tn),
        in_specs=[pl.BlockSpec((1, tm, d), lambda s, i, j: (s, i, 0)),
                  pl.BlockSpec((1, d), lambda s, i, j: (0, 0)),
                  pl.BlockSpec((1, 1, d), lambda s, i, j: (s, 0, 0)),
                  pl.BlockSpec((1, 1, d), lambda s, i, j: (s, 0, 0)),
                  pl.BlockSpec((d, tn), lambda s, i, j: (0, j))],
        out_specs=out_specs,
        out_shape=out_shape,
        scratch_shapes=[pltpu.VMEM((tm, d), BF16)],
        compiler_params=_params("parallel", "parallel", "arbitrary"),
        name="norm_matmul",
    )(x, gain, scale, shift, w)
    return res if emit_h else res[0]


def _proj_residual_kernel(of_ref, ob_ref, z_ref, hg_ref, b_ref, w_ref, x_ref, gain_ref, gate_ref, o_ref,
                          *, heads, hd, normed_first):
    ka = heads * hd
    o = of_ref[0] + ob_ref[0]
    z = z_ref[0]
    parts = []
    for h in range(heads):
        hs = slice(h * hd, (h + 1) * hd)
        zh = z[:, hs]
        parts.append((_rms_rows(o[:, hs]) * hg_ref[...] * (zh * jax.nn.sigmoid(zh))).astype(BF16))
    a = jnp.concatenate(parts, axis=-1)
    kb = w_ref.shape[0] - ka
    wa, wb = (w_ref[:ka, :], w_ref[ka:, :]) if normed_first else (w_ref[kb:, :], w_ref[:kb, :])
    y = jnp.dot(a, wa, preferred_element_type=F32) + jnp.dot(b_ref[0].astype(BF16), wb, preferred_element_type=F32)
    o_ref[0] = x_ref[0] + gate_ref[0] * (_rms_rows(y) * gain_ref[...])


def proj_residual(o_f, o_b, z, head_gain, b, w, x, gain, gate, *, heads, hd, z_block, normed_first, tm=256):
    nseg, seg_len, d = x.shape
    ka = heads * hd
    kb = d - ka

    def row_spec(width, col=0):
        return pl.BlockSpec((1, tm, width), lambda s, i: (s, i, col))

    return pl.pallas_call(
        functools.partial(_proj_residual_kernel, heads=heads, hd=hd, normed_first=normed_first),
        grid=(nseg, seg_len // tm),
        in_specs=[row_spec(ka), row_spec(ka), row_spec(ka, z_block),
                  pl.BlockSpec((1, hd), lambda s, i: (0, 0)),
                  row_spec(kb),
                  pl.BlockSpec((d, d), lambda s, i: (0, 0)),
                  row_spec(d),
                  pl.BlockSpec((1, d), lambda s, i: (0, 0)),
                  pl.BlockSpec((1, 1, d), lambda s, i: (s, 0, 0))],
        out_specs=row_spec(d),
        out_shape=jax.ShapeDtypeStruct((nseg, seg_len, d), F32),
        compiler_params=_params("parallel", "parallel"),
        name="proj_residual",
    )(o_f, o_b, z, head_gain, b, w, x, gain, gate)


def _split_hi_lo(x):
    hi = x.astype(BF16)
    lo = (x - hi.astype(F32)).astype(BF16)
    return hi, lo


GLA_LR_BLOCK = 128


def _gla_kernel(qf_ref, kf_ref, vf_ref, lrf_ref, qb_ref, kb_ref, vb_ref, lrb_ref, w2_ref, gb_ref, s0_ref,
                of_ref, ob_ref, sfin_ref, st_scr):
    c = pl.program_id(1)

    @pl.when(c == 0)
    def _():
        st_scr[...] = s0_ref[0]

    ii = lax.broadcasted_iota(jnp.int32, (CHUNK, CHUNK), 0)
    jj = lax.broadcasted_iota(jnp.int32, (CHUNK, CHUNK), 1)
    mid = CHUNK // 2
    jobs = []
    for d, (q_ref, k_ref, v_ref, lr_ref, o_ref) in enumerate(((qf_ref, kf_ref, vf_ref, lrf_ref, of_ref),
                                                              (qb_ref, kb_ref, vb_ref, lrb_ref, ob_ref))):
        incl = (ii <= jj) if d else (jj <= ii)
        tri = jnp.where(incl, 1.0, 0.0).astype(BF16)
        lr = lr_ref[0, :, :2 * GLA_GATE_RANK].astype(BF16)
        pre = jnp.dot(lr, w2_ref[d].astype(BF16), preferred_element_type=F32) + gb_ref[d]
        logd = (jnp.minimum(pre, 0.0) - jnp.log1p(jnp.exp(-jnp.abs(pre)))) / GLA_GATE_NORM
        g_hi, g_lo = _split_hi_lo(logd)
        b = jnp.dot(tri, g_hi, preferred_element_type=F32) + jnp.dot(tri, g_lo, preferred_element_type=F32)
        b_mid = b[mid - 1:mid] if d else b[mid:mid + 1]
        b_last = b[0:1] if d else b[CHUNK - 1:CHUNK]
        e_q = jnp.exp(b - b_mid)
        e_k = jnp.exp(b_mid - b)
        e_b = jnp.exp(b)
        e_kl = jnp.exp(b_last - b)
        e_l = jnp.exp(b_last)
        for h in range(GLA_HEADS):
            ks = slice(h * GLA_DK, (h + 1) * GLA_DK)
            vs = slice(h * GLA_DV, (h + 1) * GLA_DV)
            qi = q_ref[0, :, ks] * (GLA_DK ** -0.5)
            ki = k_ref[0, :, ks]
            jobs.append(dict(d=d, h=h, vs=vs, o_ref=o_ref, incl=incl,
                             qe=(qi * e_q[:, ks]).astype(BF16), ke=(ki * e_k[:, ks]).astype(BF16),
                             qb=(qi * e_b[:, ks]).astype(BF16), kd=(ki * e_kl[:, ks]).astype(BF16),
                             vi=v_ref[0, :, vs].astype(BF16), e_l=e_l[:, ks]))
    for j in jobs:
        a = lax.dot_general(j["qe"], j["ke"], NT_DIMS, preferred_element_type=F32)
        j["a"] = jnp.where(j["incl"], a, 0.0).astype(BF16)
    for j in jobs:
        j["st"] = st_scr[j["d"], j["h"]]
        j["o"] = lax.dot_general(j["qb"], j["st"].astype(BF16), NT_DIMS, preferred_element_type=F32)
    for j in jobs:
        j["o"] = j["o"] + jnp.dot(j["a"], j["vi"], preferred_element_type=F32)
    for j in jobs:
        j["st_new"] = j["st"] * j["e_l"] + jnp.dot(j["vi"].T, j["kd"], preferred_element_type=F32)
    for j in jobs:
        j["o_ref"][0, :, j["vs"]] = j["o"]
        st_scr[j["d"], j["h"]] = j["st_new"]

    @pl.when(c == pl.num_programs(1) - 1)
    def _():
        sfin_ref[0] = st_scr[...]


def gla_scan(proj, b0, bsz, w2, gb, s0):
    t = proj.shape[1]
    n = t // CHUNK
    s0t = jnp.swapaxes(s0, -1, -2)
    kw = GLA_HEADS * GLA_DK
    lr_block = (2 * kw + 2 * GLA_WIDTH + FNET_WIDTH) // GLA_LR_BLOCK

    def specs(rev):
        def cm(c):
            return n - 1 - c if rev else c
        return [pl.BlockSpec((1, CHUNK, kw), lambda b, c: (b + b0, cm(c), 0)),
                pl.BlockSpec((1, CHUNK, kw), lambda b, c: (b + b0, cm(c), 1)),
                pl.BlockSpec((1, CHUNK, GLA_WIDTH), lambda b, c: (b + b0, cm(c), 1)),
                pl.BlockSpec((1, CHUNK, GLA_LR_BLOCK), lambda b, c: (b + b0, cm(c), lr_block))]

    def out_spec(rev):
        return pl.BlockSpec((1, CHUNK, GLA_WIDTH), lambda b, c: (b, n - 1 - c if rev else c, 0))

    state_spec = pl.BlockSpec((1, 2, GLA_HEADS, GLA_DV, GLA_DK), lambda b, c: (b, 0, 0, 0, 0))
    o_f, o_b, sfin = pl.pallas_call(
        _gla_kernel,
        grid=(bsz, n),
        in_specs=specs(False) + specs(True) + [
            pl.BlockSpec((2, 2 * GLA_GATE_RANK, kw), lambda b, c: (0, 0, 0)),
            pl.BlockSpec((2, 1, kw), lambda b, c: (0, 0, 0)),
            state_spec],
        out_specs=[out_spec(False), out_spec(True), state_spec],
        out_shape=[jax.ShapeDtypeStruct((bsz, t, GLA_WIDTH), F32),
                   jax.ShapeDtypeStruct((bsz, t, GLA_WIDTH), F32),
                   jax.ShapeDtypeStruct((bsz, 2, GLA_HEADS, GLA_DV, GLA_DK), F32)],
        scratch_shapes=[pltpu.VMEM((2, GLA_HEADS, GLA_DV, GLA_DK), F32)],
        compiler_params=_params("parallel", "arbitrary"),
        name="gla_scan",
    )(*([proj] * 8), w2, gb, s0t)
    return o_f, o_b, jnp.swapaxes(sfin, -1, -2)


GDN_HALO = 8
GDN_GATE_BLOCK = 128


def _gdn_kernel(*refs):
    dir_refs = (refs[:10], refs[10:20])
    cw_ref, alog_ref, dtb_ref, s0_ref, of_ref, ob_ref, sfin_ref, s_scr, win_scr = refs[20:]
    c = pl.program_id(1)
    n = pl.num_programs(1)

    @pl.when(c == 0)
    def _():
        s_scr[...] = s0_ref[0]

    ii = lax.broadcasted_iota(jnp.int32, (CHUNK, CHUNK), 0)
    jj = lax.broadcasted_iota(jnp.int32, (CHUNK, CHUNK), 1)
    eye = jnp.where(ii == jj, 1.0, 0.0).astype(F32)

    jobs = []
    for d, o_ref in enumerate((of_ref, ob_ref)):
        chunk = n - 1 - c if d else c
        has_prev = (chunk > 0).astype(F32)
        has_next = (chunk < n - 1).astype(F32)
        pieces = []
        for p in range(3):
            cur_ref, prev_ref, next_ref = dir_refs[d][3 * p:3 * p + 3]
            win = win_scr.at[3 * d + p]
            win[GDN_HALO:GDN_HALO + CHUNK, :] = cur_ref[0]
            win[GDN_HALO - 1:GDN_HALO, :] = prev_ref[0, GDN_HALO - 1:GDN_HALO, :] * has_prev
            win[GDN_HALO + CHUNK:GDN_HALO + CHUNK + 1, :] = next_ref[0, 0:1, :] * has_next
            y = (win[GDN_HALO - 1:GDN_HALO - 1 + CHUNK, :] * cw_ref[0, p:p + 1, :]
                 + win[GDN_HALO:GDN_HALO + CHUNK, :] * cw_ref[1, p:p + 1, :]
                 + win[GDN_HALO + 1:GDN_HALO + 1 + CHUNK, :] * cw_ref[2, p:p + 1, :])
            pieces.append(y * jax.nn.sigmoid(y))
        gates = dir_refs[d][9][0]
        beta = jax.nn.sigmoid(gates[:, GDN_HEADS * d:GDN_HEADS * (d + 1)])
        ga = gates[:, GDN_HEADS * (2 + d):GDN_HEADS * (3 + d)] + dtb_ref[d:d + 1, :]
        softplus = jnp.maximum(ga, 0.0) + jnp.log1p(jnp.exp(-jnp.abs(ga)))
        logd = -jnp.exp(alog_ref[d:d + 1, :]) * softplus
        order = ii - jj if d else jj - ii
        incl = order <= 0
        strict = order < 0
        tri = jnp.where(incl, 1.0, 0.0).astype(BF16)
        g_hi, g_lo = _split_hi_lo(logd)
        dcol = jnp.dot(tri, g_hi, preferred_element_type=F32) + jnp.dot(tri, g_lo, preferred_element_type=F32)
        logd_t = jnp.concatenate([logd, jnp.zeros((CHUNK, 128 - GDN_HEADS), F32)], axis=1).T[:GDN_HEADS]
        gt_hi, gt_lo = _split_hi_lo(logd_t)
        drow = (lax.dot_general(gt_hi, tri, NT_DIMS, preferred_element_type=F32)
                + lax.dot_general(gt_lo, tri, NT_DIMS, preferred_element_type=F32))
        last = 0 if d else CHUNK - 1
        for h in range(GDN_HEADS):
            hs = slice(h * GDN_DK, (h + 1) * GDN_DK)
            qi = _l2_rows(pieces[0][:, hs]) * (GDN_DK ** -0.5)
            ki = _l2_rows(pieces[1][:, hs])
            dc = dcol[:, h:h + 1]
            bc = beta[:, h:h + 1]
            decay = jnp.exp(jnp.where(incl, dc - drow[h:h + 1, :], NEG_INF))
            kb = ki * bc
            kq = jnp.concatenate([kb, qi], axis=0).astype(BF16)
            kqk = lax.dot_general(kq, ki.astype(BF16), NT_DIMS, preferred_element_type=F32)
            a = jnp.where(strict, kqk[:CHUNK] * decay, 0.0)
            e_d = jnp.exp(dc)
            d_last = dc[last:last + 1]
            jobs.append(dict(
                d=d, h=h, hs=hs, o_ref=o_ref, p=a, tinv=eye - a, att=(kqk[CHUNK:] * decay).astype(BF16),
                rhs=jnp.concatenate([pieces[2][:, hs] * bc, kb * e_d], axis=-1).astype(BF16),
                qe=qi * e_d, kd=(ki * jnp.exp(d_last - dc)).astype(BF16), s_decay=jnp.exp(d_last)))

    for _ in range(5):
        for j in jobs:
            pb = j["p"].astype(BF16)
            j["p"] = jnp.dot(pb, pb, preferred_element_type=F32)
        for j in jobs:
            j["tinv"] = j["tinv"] + jnp.dot(j["tinv"].astype(BF16), j["p"].astype(BF16), preferred_element_type=F32)
    for j in jobs:
        j["sol"] = jnp.dot(j["tinv"].astype(BF16), j["rhs"], preferred_element_type=F32)
    for j in jobs:
        j["s"] = s_scr[j["d"], j["h"]]
        wq = jnp.concatenate([j["sol"][:, GDN_DV:], j["qe"]], axis=0).astype(BF16)
        j["ws"] = jnp.dot(wq, j["s"].astype(BF16), preferred_element_type=F32)
    for j in jobs:
        j["vnb"] = (j["sol"][:, :GDN_DV] - j["ws"][:CHUNK]).astype(BF16)
        j["o"] = j["ws"][CHUNK:] + jnp.dot(j["att"], j["vnb"], preferred_element_type=F32)
    for j in jobs:
        j["s_new"] = j["s_decay"] * j["s"] + jnp.dot(j["kd"].T, j["vnb"], preferred_element_type=F32)
    for j in jobs:
        j["o_ref"][0, :, j["hs"]] = j["o"]
        s_scr[j["d"], j["h"]] = j["s_new"]

    @pl.when(c == pl.num_programs(1) - 1)
    def _():
        sfin_ref[0] = s_scr[...]


def gdn_scan(proj, b0, bsz, conv_w, a_log, dt_bias, s0):
    t = proj.shape[1]
    n = t // CHUNK
    per = CHUNK // GDN_HALO
    gate_block = (ATT_WIDTH + 4 * GDN_WIDTH + 2 * KV_WIDTH) // GDN_GATE_BLOCK

    def specs(rev):
        def cm(c):
            return n - 1 - c if rev else c
        out = []
        for p in range(3):
            col = 1 + p
            out += [pl.BlockSpec((1, CHUNK, GDN_WIDTH), lambda b, c, col=col: (b + b0, cm(c), col)),
                    pl.BlockSpec((1, GDN_HALO, GDN_WIDTH),
                                 lambda b, c, col=col: (b + b0, jnp.maximum(cm(c) * per - 1, 0), col)),
                    pl.BlockSpec((1, GDN_HALO, GDN_WIDTH),
                                 lambda b, c, col=col: (b + b0, jnp.minimum((cm(c) + 1) * per, n * per - 1), col))]
        out.append(pl.BlockSpec((1, CHUNK, GDN_GATE_BLOCK), lambda b, c: (b + b0, cm(c), gate_block)))
        return out

    def out_spec(rev):
        return pl.BlockSpec((1, CHUNK, GDN_WIDTH), lambda b, c: (b, n - 1 - c if rev else c, 0))

    state_spec = pl.BlockSpec((1, 2, GDN_HEADS, GDN_DK, GDN_DV), lambda b, c: (b, 0, 0, 0, 0))
    o_f, o_b, sfin = pl.pallas_call(
        _gdn_kernel,
        grid=(bsz, n),
        in_specs=specs(False) + specs(True) + [
            pl.BlockSpec((CONV_W, 3, GDN_WIDTH), lambda b, c: (0, 0, 0)),
            pl.BlockSpec((2, GDN_HEADS), lambda b, c: (0, 0)),
            pl.BlockSpec((2, GDN_HEADS), lambda b, c: (0, 0)),
            state_spec],
        out_specs=[out_spec(False), out_spec(True), state_spec],
        out_shape=[jax.ShapeDtypeStruct((bsz, t, GDN_WIDTH), F32),
                   jax.ShapeDtypeStruct((bsz, t, GDN_WIDTH), F32),
                   jax.ShapeDtypeStruct((bsz, 2, GDN_HEADS, GDN_DK, GDN_DV), F32)],
        scratch_shapes=[pltpu.VMEM((2, GDN_HEADS, GDN_DK, GDN_DV), F32),
                        pltpu.VMEM((6, CHUNK + 2 * GDN_HALO, GDN_WIDTH), F32)],
        compiler_params=_params("parallel", "arbitrary"),
        name="gdn_scan",
    )(*([proj] * 20), conv_w.reshape(CONV_W, 3, GDN_WIDTH), a_log, dt_bias, s0)
    return o_f, o_b, sfin


def _attn_kernel(q_ref, k_ref, v_ref, o_ref):
    k = k_ref[0]
    v = v_ref[0]
    for g in range(ATT_GROUP):
        gs = slice(g * ATT_HD, (g + 1) * ATT_HD)
        s = lax.dot_general(q_ref[0, :, gs], k, NT_DIMS, preferred_element_type=F32) * (ATT_HD ** -0.5)
        m = jnp.max(s, axis=-1, keepdims=True)
        p = jnp.exp(s - m)
        l = jnp.sum(p, axis=-1, keepdims=True)
        o = jnp.dot(p.astype(BF16), v, preferred_element_type=F32)
        o_ref[0, :, gs] = o / l


def attention(q, k, v, *, tq=256):
    bsz, t_q, _ = q.shape
    t_k = k.shape[1]
    tq = min(tq, t_q)
    gw = ATT_GROUP * ATT_HD
    return pl.pallas_call(
        _attn_kernel,
        grid=(bsz, ATT_KV_HEADS, t_q // tq),
        in_specs=[pl.BlockSpec((1, tq, gw), lambda b, h, i: (b, i, h)),
                  pl.BlockSpec((1, t_k, ATT_HD), lambda b, h, i: (b, 0, h)),
                  pl.BlockSpec((1, t_k, ATT_HD), lambda b, h, i: (b, 0, h))],
        out_specs=pl.BlockSpec((1, tq, gw), lambda b, h, i: (b, i, h)),
        out_shape=jax.ShapeDtypeStruct((bsz, t_q, ATT_WIDTH), F32),
        compiler_params=_params("parallel", "parallel", "parallel"),
        name="attention",
    )(q, k, v)


PEER_TB = 512
PEER_ET = 1024
PEER_IT = PEER_ET // PEER_NKEYS
assert PEER_IT % 8 == 0
PEER_PAIRS = tuple((a, b) for a in range(PEER_TOPK) for b in range(PEER_TOPK) if (a + 1) * (b + 1) <= PEER_TOPK)


def _top_values(x, count):
    vals = []
    for _ in range(count):
        m = jnp.max(x, axis=0, keepdims=True)
        vals.append(m)
        x = jnp.where(x == m, NEG_INF, x)
    return vals


PEER_CAND_ROWS = _round_up(len(PEER_PAIRS), 8)


def _peer_route_kernel(q_ref, keys_ref, p1_ref, e2_ref, tau_ref, cand_scr):
    def top_candidates(rows):
        cand_scr[PEER_CAND_ROWS - 8:, :] = jnp.full((8, cand_scr.shape[1]), NEG_INF, F32)
        for r, row in enumerate(rows):
            cand_scr[r:r + 1, :] = row
        return _top_values(cand_scr[...], PEER_TOPK)

    for h in range(PEER_HEADS):
        e = []
        tops = []
        for p in range(2):
            r0 = (h * 2 + p) * PEER_HALF
            s = lax.dot_general(keys_ref[h, p], q_ref[:, r0:r0 + PEER_HALF], NT_DIMS, precision=HIGHEST,
                                preferred_element_type=F32)
            top = _top_values(s, PEER_TOPK)
            e.append(jnp.exp(s - top[0]))
            tops.append([jnp.exp(t - top[0]) for t in top])
        z = sum(top_candidates([tops[0][a] * tops[1][b] for a, b in PEER_PAIRS]))
        zinv = 1.0 / z
        p1 = e[0] * zinv
        tau = top_candidates([(tops[0][a] * zinv) * tops[1][b] for a, b in PEER_PAIRS])[-1]
        p1_ref[h] = p1
        e2_ref[h] = e[1]
        tau_ref[h] = jnp.broadcast_to(tau, (8, tau.shape[1]))


def _gelu_tanh(x):
    return 0.5 * x * (1.0 + jnp.tanh(np.sqrt(2.0 / np.pi).astype(np.float32) * (x + 0.044715 * (x * x * x))))


def _peer_dense_kernel(h_ref, u_ref, vt_ref, p1_ref, e2_ref, tau_ref, o_ref, coef_scr):
    @pl.when(pl.program_id(1) == 0)
    def _():
        o_ref[...] = jnp.zeros_like(o_ref)

    act = _gelu_tanh(lax.dot_general(u_ref[...], h_ref[...], NT_DIMS, preferred_element_type=F32))
    tb = h_ref.shape[0]
    for ts in range(tb // 128):
        ls = slice(ts * 128, (ts + 1) * 128)
        for i in range(PEER_IT):
            gate = jnp.zeros((PEER_NKEYS, 128), F32)
            for h in range(PEER_HEADS):
                w = e2_ref[h, :, ls] * p1_ref[h, i:i + 1, ls]
                gate = gate + jnp.where(w >= tau_ref[h, 0:1, ls], w, 0.0)
            rs = slice(i * PEER_NKEYS, (i + 1) * PEER_NKEYS)
            coef_scr[rs, ls] = (gate * act[rs, ls]).astype(BF16)
    o_ref[...] += jnp.dot(vt_ref[...], coef_scr[...], preferred_element_type=F32)


def _peer_residual_kernel(x_ref, yt_ref, gain_ref, gate_ref, o_ref):
    o_ref[...] = x_ref[...] + gate_ref[0] * (_rms_rows(yt_ref[...].T) * gain_ref[...])


def peer_residual(x, q, h, keys, u, v_t, gain, gate):
    nseg, seg_len, d = x.shape
    n = nseg * seg_len
    nb = n // PEER_TB
    p1, e2, tau = pl.pallas_call(
        _peer_route_kernel,
        grid=(nb,),
        in_specs=[pl.BlockSpec((PEER_TB, PEER_HEADS * 2 * PEER_HALF), lambda i: (i, 0)),
                  pl.BlockSpec((PEER_HEADS, 2, PEER_NKEYS, PEER_HALF), lambda i: (0, 0, 0, 0))],
        out_specs=[pl.BlockSpec((PEER_HEADS, PEER_NKEYS, PEER_TB), lambda i: (0, 0, i)),
                   pl.BlockSpec((PEER_HEADS, PEER_NKEYS, PEER_TB), lambda i: (0, 0, i)),
                   pl.BlockSpec((PEER_HEADS, 8, PEER_TB), lambda i: (0, 0, i))],
        out_shape=[jax.ShapeDtypeStruct((PEER_HEADS, PEER_NKEYS, n), F32),
                   jax.ShapeDtypeStruct((PEER_HEADS, PEER_NKEYS, n), F32),
                   jax.ShapeDtypeStruct((PEER_HEADS, 8, n), F32)],
        scratch_shapes=[pltpu.VMEM((PEER_CAND_ROWS, PEER_TB), F32)],
        compiler_params=_params("parallel"),
        name="peer_route",
    )(q.reshape(n, -1), keys)
    n_tiles = N_EXPERTS // PEER_ET
    y_t = pl.pallas_call(
        _peer_dense_kernel,
        grid=(nb, n_tiles),
        in_specs=[pl.BlockSpec((PEER_TB, d), lambda i, e: (i, 0)),
                  pl.BlockSpec((PEER_ET, d), lambda i, e: (e, 0)),
                  pl.BlockSpec((d, PEER_ET), lambda i, e: (0, e)),
                  pl.BlockSpec((PEER_HEADS, PEER_IT, PEER_TB), lambda i, e: (0, e, i)),
                  pl.BlockSpec((PEER_HEADS, PEER_NKEYS, PEER_TB), lambda i, e: (0, 0, i)),
                  pl.BlockSpec((PEER_HEADS, 8, PEER_TB), lambda i, e: (0, 0, i))],
        out_specs=pl.BlockSpec((d, PEER_TB), lambda i, e: (0, i)),
        out_shape=jax.ShapeDtypeStruct((d, n), F32),
        scratch_shapes=[pltpu.VMEM((PEER_ET, PEER_TB), BF16)],
        compiler_params=_params("parallel", "arbitrary"),
        name="peer_dense",
    )(h.reshape(n, d), u, v_t, p1, e2, tau)
    tr = 256
    per_seg = seg_len // tr
    return pl.pallas_call(
        _peer_residual_kernel,
        grid=(n // tr,),
        in_specs=[pl.BlockSpec((tr, d), lambda i: (i, 0)),
                  pl.BlockSpec((d, tr), lambda i: (0, i)),
                  pl.BlockSpec((1, d), lambda i: (0, 0)),
                  pl.BlockSpec((1, 1, d), lambda i: (i // per_seg, 0, 0))],
        out_specs=pl.BlockSpec((tr, d), lambda i: (i, 0)),
        out_shape=jax.ShapeDtypeStruct((n, d), F32),
        compiler_params=_params("parallel"),
        name="peer_residual",
    )(x.reshape(n, d), y_t, gain, gate).reshape(nseg, seg_len, d)


def _rmsnorm(x, g):
    return x * lax.rsqrt(jnp.mean(x * x, axis=-1, keepdims=True) + EPS) * g


def _dft_tables(t):
    c = 1 << (int(np.log2(t)) // 2)
    r = t // c
    k = jnp.arange(t, dtype=jnp.int32)

    def angles(rows):
        return ((rows[:, None] * k[None, :]) % t).astype(F32) * (2.0 * np.pi / t)

    hi = angles(jnp.arange(r, dtype=jnp.int32) * c)[:, None, :]
    lo = angles(jnp.arange(c, dtype=jnp.int32))[None, :, :]
    cos = jnp.cos(hi) * jnp.cos(lo) - jnp.sin(hi) * jnp.sin(lo)
    sin = jnp.sin(hi) * jnp.cos(lo) + jnp.cos(hi) * jnp.sin(lo)
    return cos.reshape(t, t), sin.reshape(t, t)


def fnet_mix(xf, n_ctx_seq, ctx_len):
    nseg, seg_len, _ = xf.shape
    cc, sc = _dft_tables(FNET_CH)
    eye = jnp.eye(FNET_GROUPS, dtype=F32)
    w_ch = jnp.concatenate([jnp.kron(eye, cc), jnp.kron(eye, sc)], axis=1)
    z = matmul(xf.reshape(nseg * seg_len, FNET_WIDTH), w_ch).reshape(nseg, seg_len, 2 * FNET_WIDTH)

    def seq_dft(zz, t):
        bsz = zz.shape[0]
        ct, st = _dft_tables(t)
        scale = (t * FNET_CH) ** -0.5
        w_t = jnp.concatenate([ct, -st], axis=1) * scale
        stacked = jnp.concatenate([zz[..., :FNET_WIDTH], zz[..., FNET_WIDTH:]], axis=1)
        rhs = jnp.swapaxes(stacked, 0, 1).reshape(2 * t, bsz * FNET_WIDTH)
        y = matmul(w_t.astype(BF16), rhs.astype(BF16), tm=1024, tn=1024)
        return jnp.swapaxes(y.reshape(t, bsz, FNET_WIDTH), 0, 1)

    y_ctx = seq_dft(z[0].reshape(n_ctx_seq, ctx_len, 2 * FNET_WIDTH), ctx_len).reshape(1, seg_len, FNET_WIDTH)
    y_lat = seq_dft(z[1:], seg_len)
    return jnp.concatenate([y_ctx, y_lat], axis=0)


def _axial_rope(x):
    t = x.shape[1]
    rows = t // GRID_W
    row = jnp.repeat(jnp.arange(rows, dtype=F32), GRID_W)
    col = jnp.tile(jnp.arange(GRID_W, dtype=F32), rows)
    freqs = ROPE_THETA ** (-jnp.arange(AXIS_DIM // 2, dtype=F32) * 2.0 / AXIS_DIM)

    def rot(xa, pos):
        ang = pos[:, None] * freqs[None, :]
        cos, sin = jnp.cos(ang)[None, :, None, :], jnp.sin(ang)[None, :, None, :]
        x1, x2 = xa[..., :AXIS_DIM // 2], xa[..., AXIS_DIM // 2:]
        return jnp.concatenate([x1 * cos - x2 * sin, x2 * cos + x1 * sin], axis=-1)

    return jnp.concatenate([rot(x[..., :AXIS_DIM], row), rot(x[..., AXIS_DIM:], col)], axis=-1)


EVEN_COLS = (GLA_HEADS * GLA_DK, GLA_HEADS * GLA_DK, GLA_WIDTH, GLA_WIDTH, FNET_WIDTH, GLA_GATE_RANK, GLA_GATE_RANK)
ODD_COLS = (ATT_WIDTH, GDN_WIDTH, GDN_WIDTH, GDN_WIDTH, GDN_WIDTH, KV_WIDTH, KV_WIDTH,
            GDN_HEADS, GDN_HEADS, GDN_HEADS, GDN_HEADS)
PROJ_TN = 768


def _pad_cols(w):
    return jnp.pad(w, ((0, 0), (0, _round_up(w.shape[1], PROJ_TN) - w.shape[1])))


def _even_weight(w_in):
    q, k, v, g, lr_f, lr_b, xf = _split_cols(w_in, EVEN_SPLITS)
    return _pad_cols(jnp.concatenate([q, k, v, g, xf, lr_f, lr_b], axis=1)).astype(BF16)


def _odd_weight(w_in):
    qa, ka, va, qkv, z, b_f, b_b, a_f, a_b = _split_cols(w_in, ODD_SPLITS)
    return _pad_cols(jnp.concatenate([qa, qkv, z, ka, va, b_f, b_b, a_f, a_b], axis=1)).astype(BF16)


def _even_mix(proj, n_ctx, ctx_len, gate_w2, gate_b, state):
    nseg, seg_len, width = proj.shape
    xf = proj[..., sum(EVEN_COLS[:4]):sum(EVEN_COLS[:5])]
    zeros = jnp.zeros_like(gate_w2[0])
    w2 = jnp.stack([jnp.concatenate([gate_w2[0], zeros]), jnp.concatenate([zeros, gate_w2[1]])])
    gb = gate_b[:, None, :]
    zero = jnp.zeros((n_ctx, 2, GLA_HEADS, GLA_DK, GLA_DV), F32)
    of_c, ob_c, s_ctx = gla_scan(proj.reshape(nseg * n_ctx, ctx_len, width), 0, n_ctx, w2, gb, zero)
    of_l, ob_l, _ = gla_scan(proj, 1, nseg - 1, w2, gb, state)

    def cat(a, b):
        return jnp.concatenate([a.reshape(1, seg_len, GLA_WIDTH), b], axis=0)
    return cat(of_c, of_l), cat(ob_c, ob_l), fnet_mix(xf, n_ctx, ctx_len), s_ctx


def _odd_mix(proj, n_ctx, ctx_len, qk_g, conv_w, a_log, dt_bias, ctx_k, ctx_v, state):
    nseg, seg_len, width = proj.shape
    qa = proj[..., :ATT_WIDTH]
    ka, va = _split_cols(proj[..., sum(ODD_COLS[:5]):sum(ODD_COLS[:7])], ODD_COLS[5:7])

    def attend(sl, bsz, t, latent):
        def r(a):
            return a[sl].reshape(bsz, t, a.shape[-1])
        q4 = _rmsnorm(r(qa).reshape(bsz, t, ATT_HEADS, ATT_HD), qk_g[0])
        k4 = _rmsnorm(r(ka).reshape(bsz, t, ATT_KV_HEADS, ATT_HD), qk_g[1])
        v4 = r(va).reshape(bsz, t, ATT_KV_HEADS, ATT_HD)
        if latent:
            q4, k4 = _axial_rope(q4), _axial_rope(k4)
            k_all = jnp.concatenate([ctx_k, k4], axis=1)
            v_all = jnp.concatenate([ctx_v, v4], axis=1)
        else:
            k_all, v_all = k4, v4
        t_k = k_all.shape[1]
        o_att = attention(q4.reshape(bsz, t, ATT_WIDTH).astype(BF16),
                          k_all.reshape(bsz, t_k, KV_WIDTH).astype(BF16),
                          v_all.reshape(bsz, t_k, KV_WIDTH).astype(BF16))
        return o_att.reshape(-1, seg_len, ATT_WIDTH), k4, v4

    att_c, k_ctx, v_ctx = attend(slice(0, 1), n_ctx, ctx_len, False)
    att_l, _, _ = attend(slice(1, None), nseg - 1, seg_len, True)
    zero = jnp.zeros((n_ctx, 2, GDN_HEADS, GDN_DK, GDN_DV), F32)
    of_c, ob_c, s_ctx = gdn_scan(proj.reshape(nseg * n_ctx, ctx_len, width), 0, n_ctx, conv_w, a_log, dt_bias, zero)
    of_l, ob_l, _ = gdn_scan(proj, 1, nseg - 1, conv_w, a_log, dt_bias, state)

    def cat(a, b):
        return jnp.concatenate([a.reshape(1, seg_len, a.shape[-1]), b], axis=0)
    return cat(of_c, of_l), cat(ob_c, ob_l), cat(att_c, att_l), k_ctx, v_ctx, s_ctx


def kernel(x_prompt, x_sample, cache_k, cache_v, state_gla, state_delta, c, c_ctx, ada_w, ada_b, norm_g, w_out, even_w_in, gla_gate_w2, gla_gate_b, gla_out_norm, odd_w_in, qk_norm, conv_w, gdn_a_log, gdn_dt_bias, gdn_out_norm, peer_wq, peer_keys, peer_u, peer_v):
    n_ctx, ctx_len, d = x_prompt.shape
    n_lat, seg_len, _ = x_sample.shape
    assert n_ctx * ctx_len == seg_len and d == D_MODEL
    depth = ada_w.shape[0]
    x = jnp.concatenate([x_prompt.reshape(1, seg_len, d), x_sample], axis=0)
    cvec = jnp.concatenate([c_ctx[None, :], c], axis=0)
    new_gla, new_delta, new_k, new_v = [], [], [], []
    for l in range(depth):
        i = l // 2
        mod = matmul(jax.nn.silu(cvec), ada_w[l], tn=1024) + ada_b[l]
        sh1, sc1, g1, sh2, sc2, g2 = [m[:, None, :] for m in jnp.split(mod, 6, axis=-1)]
        gains = norm_g[l][:, None, :]
        w_o = w_out[l].astype(BF16)
        if l % 2 == 0:
            proj = norm_matmul(x, gains[0], 1 + sc1, sh1, _even_weight(even_w_in[i]), tm=1024, tn=PROJ_TN)
            o_f, o_b, other, s_new = _even_mix(proj, n_ctx, ctx_len, gla_gate_w2[i], gla_gate_b[i], state_gla[:, i])
            new_gla.append(s_new)
            x = proj_residual(o_f, o_b, proj, gla_out_norm[i][None, :], other, w_o, x, gains[1], g1,
                              heads=GLA_HEADS, hd=GLA_DV, z_block=2, normed_first=True)
        else:
            proj = norm_matmul(x, gains[0], 1 + sc1, sh1, _odd_weight(odd_w_in[i]), tm=1024, tn=PROJ_TN)
            o_f, o_b, other, k_new, v_new, s_new = _odd_mix(proj, n_ctx, ctx_len, qk_norm[i], conv_w[i], gdn_a_log[i],
                                                            gdn_dt_bias[i], cache_k[:, i], cache_v[:, i],
                                                            state_delta[:, i])
            new_k.append(k_new)
            new_v.append(v_new)
            new_delta.append(s_new)
            x = proj_residual(o_f, o_b, proj, gdn_out_norm[i][None, :], other, w_o, x, gains[1], g1,
                              heads=GDN_HEADS, hd=GDN_DV, z_block=4, normed_first=False)
        q, h = norm_matmul(x, gains[2], 1 + sc2, sh2, peer_wq[l].astype(BF16), tm=512, tn=1024, emit_h=True)
        x = peer_residual(x, q, h, peer_keys[l], peer_u[l].astype(BF16), peer_v[l].T.astype(BF16), gains[3], g2)
    return (x[0].reshape(n_ctx, ctx_len, d), x[1:],
            jnp.stack(new_k, axis=1), jnp.stack(new_v, axis=1),
            jnp.stack(new_gla, axis=1), jnp.stack(new_delta, axis=1))
```

```python
import functools

import numpy as np
import jax
import jax.numpy as jnp
from jax import lax
from jax.experimental import pallas as pl
from jax.experimental.pallas import tpu as pltpu

F32 = jnp.float32
BF16 = jnp.bfloat16
HIGHEST = lax.Precision.HIGHEST

D_MODEL = 2048
EPS = 1e-6
CHUNK = 64

GLA_HEADS = 6
GLA_DK = 128
GLA_DV = 256
GLA_WIDTH = GLA_HEADS * GLA_DV
GLA_GATE_RANK = 16
GLA_GATE_NORM = 16.0
FNET_WIDTH = D_MODEL - GLA_WIDTH
FNET_GROUPS = 4
FNET_CH = FNET_WIDTH // FNET_GROUPS

ATT_HD = 128
ATT_HEADS = 8
ATT_KV_HEADS = 2
ATT_GROUP = ATT_HEADS // ATT_KV_HEADS
ATT_WIDTH = ATT_HEADS * ATT_HD
KV_WIDTH = ATT_KV_HEADS * ATT_HD
AXIS_DIM = ATT_HD // 2
ROPE_THETA = 10000.0
GRID_W = 64
GDN_HEADS = 8
GDN_DK = 128
GDN_DV = 128
GDN_WIDTH = GDN_HEADS * GDN_DV
CONV_W = 3

PEER_HEADS = 8
PEER_NKEYS = 128
PEER_HALF = 128
PEER_TOPK = 16
N_EXPERTS = PEER_NKEYS ** 2

EVEN_SPLITS = (GLA_HEADS * GLA_DK, GLA_HEADS * GLA_DK, GLA_WIDTH, GLA_WIDTH, GLA_GATE_RANK, GLA_GATE_RANK, FNET_WIDTH)
ODD_SPLITS = (ATT_WIDTH, KV_WIDTH, KV_WIDTH, 3 * GDN_WIDTH, GDN_WIDTH, GDN_HEADS, GDN_HEADS, GDN_HEADS, GDN_HEADS)

VMEM_LIMIT_BYTES = 56 * 1024 * 1024
NEG_INF = float("-inf")

NT_DIMS = (((1,), (1,)), ((), ()))


def _params(*semantics):
    return pltpu.CompilerParams(dimension_semantics=semantics, vmem_limit_bytes=VMEM_LIMIT_BYTES)


def _split_cols(x, sizes):
    return jnp.split(x, [int(s) for s in np.cumsum(sizes)[:-1]], axis=-1)


def _round_up(n, m):
    return -(-n // m) * m


def _mm_kernel(x_ref, w_ref, o_ref, acc_ref):
    k = pl.program_id(2)

    @pl.when(k == 0)
    def _():
        acc_ref[...] = jnp.zeros_like(acc_ref)

    acc_ref[...] += jnp.dot(x_ref[...].astype(BF16), w_ref[...].astype(BF16), preferred_element_type=F32)

    @pl.when(k == pl.num_programs(2) - 1)
    def _():
        o_ref[...] = acc_ref[...]


def _pick_tile(n, pref):
    for t in (pref, 1024, 512, 256, 128):
        if t <= pref and n % t == 0:
            return t
    return n


def matmul(x, w, *, tm=1024, tn=512, tk=2048):
    m, kdim = x.shape
    n = w.shape[1]
    m_pad = _round_up(m, 8)
    n_pad = _round_up(n, 256)
    if m_pad != m:
        x = jnp.pad(x, ((0, m_pad - m), (0, 0)))
    if n_pad != n:
        w = jnp.pad(w, ((0, 0), (0, n_pad - n)))
    tm = _pick_tile(m_pad, tm)
    tn = _pick_tile(n_pad, tn)
    tk = _pick_tile(kdim, tk)
    out = pl.pallas_call(
        _mm_kernel,
        grid=(m_pad // tm, n_pad // tn, kdim // tk),
        in_specs=[pl.BlockSpec((tm, tk), lambda i, j, k: (i, k)),
                  pl.BlockSpec((tk, tn), lambda i, j, k: (k, j))],
        out_specs=pl.BlockSpec((tm, tn), lambda i, j, k: (i, j)),
        out_shape=jax.ShapeDtypeStruct((m_pad, n_pad), F32),
        scratch_shapes=[pltpu.VMEM((tm, tn), F32)],
        compiler_params=_params("parallel", "parallel", "arbitrary"),
        name="matmul",
    )(x, w)
    return out[:m, :n]


def _rms_rows(x):
    return x * lax.rsqrt(jnp.mean(x * x, axis=-1, keepdims=True) + EPS)


def _l2_rows(x):
    return x * lax.rsqrt(jnp.sum(x * x, axis=-1, keepdims=True) + EPS)


def _norm_matmul_kernel(x_ref, gain_ref, scale_ref, shift_ref, w_ref, *rest, emit_h):
    if emit_h:
        o_ref, h_ref, h_scr = rest
    else:
        o_ref, h_scr = rest

    @pl.when(pl.program_id(2) == 0)
    def _():
        h = _rms_rows(x_ref[0]) * (gain_ref[...] * scale_ref[0]) + shift_ref[0]
        h_scr[...] = h.astype(BF16)
        if emit_h:
            h_ref[0] = h_scr[...]

    o_ref[0] = jnp.dot(h_scr[...], w_ref[...], preferred_element_type=F32)


def norm_matmul(x, gain, scale, shift, w, *, tm, tn, emit_h=False):
    nseg, seg_len, d = x.shape
    n = w.shape[1]
    tm = min(tm, seg_len)
    out_shape = [jax.ShapeDtypeStruct((nseg, seg_len, n), F32)]
    out_specs = [pl.BlockSpec((1, tm, tn), lambda s, i, j: (s, i, j))]
    if emit_h:
        out_shape.append(jax.ShapeDtypeStruct((nseg, seg_len, d), BF16))
        out_specs.append(pl.BlockSpec((1, tm, d), lambda s, i, j: (s, i, 0)))
    res = pl.pallas_call(
        functools.partial(_norm_matmul_kernel, emit_h=emit_h),
        grid=(nseg, seg_len // tm, n // tn),
        in_specs=[pl.BlockSpec((1, tm, d), lambda s, i, j: (s, i, 0)),
                  pl.BlockSpec((1, d), lambda s, i, j: (0, 0)),
                  pl.BlockSpec((1, 1, d), lambda s, i, j: (s, 0, 0)),
                  pl.BlockSpec((1, 1, d), lambda s, i, j: (s, 0, 0)),
                  pl.BlockSpec((d, tn), lambda s, i, j: (0, j))],
        out_specs=out_specs,
        out_shape=out_shape,
        scratch_shapes=[pltpu.VMEM((tm, d), BF16)],
        compiler_params=_params("parallel", "parallel", "arbitrary"),
        name="norm_matmul",
    )(x, gain, scale, shift, w)
    return res if emit_h else res[0]


def _proj_residual_kernel(of_ref, ob_ref, z_ref, hg_ref, b_ref, w_ref, x_ref, gain_ref, gate_ref, o_ref,
                          *, heads, hd, normed_first):
    ka = heads * hd
    o = of_ref[0] + ob_ref[0]
    z = z_ref[0]
    parts = []
    for h in range(heads):
        hs = slice(h * hd, (h + 1) * hd)
        zh = z[:, hs]
        parts.append((_rms_rows(o[:, hs]) * hg_ref[...] * (zh * jax.nn.sigmoid(zh))).astype(BF16))
    a = jnp.concatenate(parts, axis=-1)
    kb = w_ref.shape[0] - ka
    wa, wb = (w_ref[:ka, :], w_ref[ka:, :]) if normed_first else (w_ref[kb:, :], w_ref[:kb, :])
    y = jnp.dot(a, wa, preferred_element_type=F32) + jnp.dot(b_ref[0].astype(BF16), wb, preferred_element_type=F32)
    o_ref[0] = x_ref[0] + gate_ref[0] * (_rms_rows(y) * gain_ref[...])


def proj_residual(o_f, o_b, z, head_gain, b, w, x, gain, gate, *, heads, hd, z_block, normed_first, tm=256):
    nseg, seg_len, d = x.shape
    ka = heads * hd
    kb = d - ka

    def row_spec(width, col=0):
        return pl.BlockSpec((1, tm, width), lambda s, i: (s, i, col))

    return pl.pallas_call(
        functools.partial(_proj_residual_kernel, heads=heads, hd=hd, normed_first=normed_first),
        grid=(nseg, seg_len // tm),
        in_specs=[row_spec(ka), row_spec(ka), row_spec(ka, z_block),
                  pl.BlockSpec((1, hd), lambda s, i: (0, 0)),
                  row_spec(kb),
                  pl.BlockSpec((d, d), lambda s, i: (0, 0)),
                  row_spec(d),
                  pl.BlockSpec((1, d), lambda s, i: (0, 0)),
                  pl.BlockSpec((1, 1, d), lambda s, i: (s, 0, 0))],
        out_specs=row_spec(d),
        out_shape=jax.ShapeDtypeStruct((nseg, seg_len, d), F32),
        compiler_params=_params("parallel", "parallel"),
        name="proj_residual",
    )(o_f, o_b, z, head_gain, b, w, x, gain, gate)


def _split_hi_lo(x):
    hi = x.astype(BF16)
    lo = (x - hi.astype(F32)).astype(BF16)
    return hi, lo


GLA_LR_BLOCK = 128


def _gla_kernel(qf_ref, kf_ref, vf_ref, lrf_ref, qb_ref, kb_ref, vb_ref, lrb_ref, w2_ref, gb_ref, s0_ref, *rest):
    of_ref, ob_ref, sfin_ref, st_scr = rest[-4:]
    c = pl.program_id(1)

    @pl.when(c == 0)
    def _():
        st_scr[...] = s0_ref[0]

    ii = lax.broadcasted_iota(jnp.int32, (CHUNK, CHUNK), 0)
    jj = lax.broadcasted_iota(jnp.int32, (CHUNK, CHUNK), 1)
    mid = CHUNK // 2
    jobs = []
    for d, (q_ref, k_ref, v_ref, lr_ref, o_ref) in enumerate(((qf_ref, kf_ref, vf_ref, lrf_ref, of_ref),
                                                              (qb_ref, kb_ref, vb_ref, lrb_ref, ob_ref))):
        incl = (ii <= jj) if d else (jj <= ii)
        tri = jnp.where(incl, 1.0, 0.0).astype(BF16)
        lr = lr_ref[0, :, :2 * GLA_GATE_RANK].astype(BF16)
        pre = jnp.dot(lr, w2_ref[d].astype(BF16), preferred_element_type=F32) + gb_ref[d]
        logd = (jnp.minimum(pre, 0.0) - jnp.log1p(jnp.exp(-jnp.abs(pre)))) / GLA_GATE_NORM
        g_hi, g_lo = _split_hi_lo(logd)
        b = jnp.dot(tri, g_hi, preferred_element_type=F32) + jnp.dot(tri, g_lo, preferred_element_type=F32)
        b_mid = b[mid - 1:mid] if d else b[mid:mid + 1]
        b_last = b[0:1] if d else b[CHUNK - 1:CHUNK]
        e_q = jnp.exp(b - b_mid)
        e_k = jnp.exp(b_mid - b)
        e_b = jnp.exp(b)
        e_kl = jnp.exp(b_last - b)
        e_l = jnp.exp(b_last)
        for h in range(GLA_HEADS):
            ks = slice(h * GLA_DK, (h + 1) * GLA_DK)
            vs = slice(h * GLA_DV, (h + 1) * GLA_DV)
            qi = q_ref[0, :, ks] * (GLA_DK ** -0.5)
            ki = k_ref[0, :, ks]
            jobs.append(dict(d=d, h=h, vs=vs, o_ref=o_ref, incl=incl,
                             qe=(qi * e_q[:, ks]).astype(BF16), ke=(ki * e_k[:, ks]).astype(BF16),
                             qb=(qi * e_b[:, ks]).astype(BF16), kd=(ki * e_kl[:, ks]).astype(BF16),
                             vi=v_ref[0, :, vs].astype(BF16), e_l=e_l[:, ks]))
    for j in jobs:
        a = lax.dot_general(j["qe"], j["ke"], NT_DIMS, preferred_element_type=F32)
        j["a"] = jnp.where(j["incl"], a, 0.0).astype(BF16)
    for j in jobs:
        j["st"] = st_scr[j["d"], j["h"]]
        j["o"] = lax.dot_general(j["qb"], j["st"].astype(BF16), NT_DIMS, preferred_element_type=F32)
    for j in jobs:
        j["o"] = j["o"] + jnp.dot(j["a"], j["vi"], preferred_element_type=F32)
    for j in jobs:
        j["st_new"] = j["st"] * j["e_l"] + jnp.dot(j["vi"].T, j["kd"], preferred_element_type=F32)
    for j in jobs:
        j["o_ref"][0, :, j["vs"]] = j["o"]
        st_scr[j["d"], j["h"]] = j["st_new"]

    @pl.when(c == pl.num_programs(1) - 1)
    def _():
        sfin_ref[0] = st_scr[...]


def gla_scan(proj, b0, bsz, w2, gb, s0, into=()):
    nb_total, t, _ = proj.shape
    n = t // CHUNK
    s0t = jnp.swapaxes(s0, -1, -2)
    kw = GLA_HEADS * GLA_DK
    lr_block = (2 * kw + 2 * GLA_WIDTH + FNET_WIDTH) // GLA_LR_BLOCK

    def specs(rev):
        def cm(c):
            return n - 1 - c if rev else c
        return [pl.BlockSpec((1, CHUNK, kw), lambda b, c: (b + b0, cm(c), 0)),
                pl.BlockSpec((1, CHUNK, kw), lambda b, c: (b + b0, cm(c), 1)),
                pl.BlockSpec((1, CHUNK, GLA_WIDTH), lambda b, c: (b + b0, cm(c), 1)),
                pl.BlockSpec((1, CHUNK, GLA_LR_BLOCK), lambda b, c: (b + b0, cm(c), lr_block))]

    def out_spec(rev):
        return pl.BlockSpec((1, CHUNK, GLA_WIDTH), lambda b, c: (b + b0, n - 1 - c if rev else c, 0))

    state_spec = pl.BlockSpec((1, 2, GLA_HEADS, GLA_DV, GLA_DK), lambda b, c: (b, 0, 0, 0, 0))
    into = tuple(a.reshape(nb_total, t, GLA_WIDTH) for a in into)
    o_shape = jax.ShapeDtypeStruct((nb_total, t, GLA_WIDTH), F32)
    o_f, o_b, sfin = pl.pallas_call(
        _gla_kernel,
        grid=(bsz, n),
        in_specs=specs(False) + specs(True) + [
            pl.BlockSpec((2, 2 * GLA_GATE_RANK, kw), lambda b, c: (0, 0, 0)),
            pl.BlockSpec((2, 1, kw), lambda b, c: (0, 0, 0)),
            state_spec] + [pl.BlockSpec(memory_space=pl.ANY)] * len(into),
        out_specs=[out_spec(False), out_spec(True), state_spec],
        out_shape=[o_shape, o_shape, jax.ShapeDtypeStruct((bsz, 2, GLA_HEADS, GLA_DV, GLA_DK), F32)],
        input_output_aliases={11 + k: k for k in range(len(into))},
        scratch_shapes=[pltpu.VMEM((2, GLA_HEADS, GLA_DV, GLA_DK), F32)],
        compiler_params=_params("parallel", "arbitrary"),
        name="gla_scan",
    )(*([proj] * 8), w2, gb, s0t, *into)
    return o_f, o_b, jnp.swapaxes(sfin, -1, -2)


GDN_HALO = 8
GDN_GATE_BLOCK = 128


def _gdn_kernel(*refs):
    dir_refs = (refs[:10], refs[10:20])
    cw_ref, alog_ref, dtb_ref, s0_ref = refs[20:24]
    of_ref, ob_ref, sfin_ref, s_scr, win_scr = refs[-5:]
    c = pl.program_id(1)
    n = pl.num_programs(1)

    @pl.when(c == 0)
    def _():
        s_scr[...] = s0_ref[0]

    ii = lax.broadcasted_iota(jnp.int32, (CHUNK, CHUNK), 0)
    jj = lax.broadcasted_iota(jnp.int32, (CHUNK, CHUNK), 1)
    eye = jnp.where(ii == jj, 1.0, 0.0).astype(F32)

    jobs = []
    for d, o_ref in enumerate((of_ref, ob_ref)):
        chunk = n - 1 - c if d else c
        has_prev = (chunk > 0).astype(F32)
        has_next = (chunk < n - 1).astype(F32)
        pieces = []
        for p in range(3):
            cur_ref, prev_ref, next_ref = dir_refs[d][3 * p:3 * p + 3]
            win = win_scr.at[3 * d + p]
            win[GDN_HALO:GDN_HALO + CHUNK, :] = cur_ref[0]
            win[GDN_HALO - 1:GDN_HALO, :] = prev_ref[0, GDN_HALO - 1:GDN_HALO, :] * has_prev
            win[GDN_HALO + CHUNK:GDN_HALO + CHUNK + 1, :] = next_ref[0, 0:1, :] * has_next
            y = (win[GDN_HALO - 1:GDN_HALO - 1 + CHUNK, :] * cw_ref[0, p:p + 1, :]
                 + win[GDN_HALO:GDN_HALO + CHUNK, :] * cw_ref[1, p:p + 1, :]
                 + win[GDN_HALO + 1:GDN_HALO + 1 + CHUNK, :] * cw_ref[2, p:p + 1, :])
            pieces.append(y * jax.nn.sigmoid(y))
        gates = dir_refs[d][9][0]
        beta = jax.nn.sigmoid(gates[:, GDN_HEADS * d:GDN_HEADS * (d + 1)])
        ga = gates[:, GDN_HEADS * (2 + d):GDN_HEADS * (3 + d)] + dtb_ref[d:d + 1, :]
        softplus = jnp.maximum(ga, 0.0) + jnp.log1p(jnp.exp(-jnp.abs(ga)))
        logd = -jnp.exp(alog_ref[d:d + 1, :]) * softplus
        order = ii - jj if d else jj - ii
        incl = order <= 0
        strict = order < 0
        tri = jnp.where(incl, 1.0, 0.0).astype(BF16)
        g_hi, g_lo = _split_hi_lo(logd)
        dcol = jnp.dot(tri, g_hi, preferred_element_type=F32) + jnp.dot(tri, g_lo, preferred_element_type=F32)
        logd_t = jnp.concatenate([logd, jnp.zeros((CHUNK, 128 - GDN_HEADS), F32)], axis=1).T[:GDN_HEADS]
        gt_hi, gt_lo = _split_hi_lo(logd_t)
        drow = (lax.dot_general(gt_hi, tri, NT_DIMS, preferred_element_type=F32)
                + lax.dot_general(gt_lo, tri, NT_DIMS, preferred_element_type=F32))
        last = 0 if d else CHUNK - 1
        for h in range(GDN_HEADS):
            hs = slice(h * GDN_DK, (h + 1) * GDN_DK)
            qi = _l2_rows(pieces[0][:, hs]) * (GDN_DK ** -0.5)
            ki = _l2_rows(pieces[1][:, hs])
            dc = dcol[:, h:h + 1]
            bc = beta[:, h:h + 1]
            decay = jnp.exp(jnp.where(incl, dc - drow[h:h + 1, :], NEG_INF))
            kb = ki * bc
            kq = jnp.concatenate([kb, qi], axis=0).astype(BF16)
            kqk = lax.dot_general(kq, ki.astype(BF16), NT_DIMS, preferred_element_type=F32)
            a = jnp.where(strict, kqk[:CHUNK] * decay, 0.0)
            e_d = jnp.exp(dc)
            d_last = dc[last:last + 1]
            jobs.append(dict(
                d=d, h=h, hs=hs, o_ref=o_ref, p=a, tinv=eye - a, att=(kqk[CHUNK:] * decay).astype(BF16),
                rhs=jnp.concatenate([pieces[2][:, hs] * bc, kb * e_d], axis=-1).astype(BF16),
                qe=qi * e_d, kd=(ki * jnp.exp(d_last - dc)).astype(BF16), s_decay=jnp.exp(d_last)))

    for _ in range(5):
        for j in jobs:
            pb = j["p"].astype(BF16)
            j["p"] = jnp.dot(pb, pb, preferred_element_type=F32)
        for j in jobs:
            j["tinv"] = j["tinv"] + jnp.dot(j["tinv"].astype(BF16), j["p"].astype(BF16), preferred_element_type=F32)
    for j in jobs:
        j["sol"] = jnp.dot(j["tinv"].astype(BF16), j["rhs"], preferred_element_type=F32)
    for j in jobs:
        j["s"] = s_scr[j["d"], j["h"]]
        wq = jnp.concatenate([j["sol"][:, GDN_DV:], j["qe"]], axis=0).astype(BF16)
        j["ws"] = jnp.dot(wq, j["s"].astype(BF16), preferred_element_type=F32)
    for j in jobs:
        j["vnb"] = (j["sol"][:, :GDN_DV] - j["ws"][:CHUNK]).astype(BF16)
        j["o"] = j["ws"][CHUNK:] + jnp.dot(j["att"], j["vnb"], preferred_element_type=F32)
    for j in jobs:
        j["s_new"] = j["s_decay"] * j["s"] + jnp.dot(j["kd"].T, j["vnb"], preferred_element_type=F32)
    for j in jobs:
        j["o_ref"][0, :, j["hs"]] = j["o"]
        s_scr[j["d"], j["h"]] = j["s_new"]

    @pl.when(c == pl.num_programs(1) - 1)
    def _():
        sfin_ref[0] = s_scr[...]


def gdn_scan(proj, b0, bsz, conv_w, a_log, dt_bias, s0, into=()):
    nb_total, t, _ = proj.shape
    n = t // CHUNK
    per = CHUNK // GDN_HALO
    gate_block = (ATT_WIDTH + 4 * GDN_WIDTH + 2 * KV_WIDTH) // GDN_GATE_BLOCK

    def specs(rev):
        def cm(c):
            return n - 1 - c if rev else c
        out = []
        for p in range(3):
            col = 1 + p
            out += [pl.BlockSpec((1, CHUNK, GDN_WIDTH), lambda b, c, col=col: (b + b0, cm(c), col)),
                    pl.BlockSpec((1, GDN_HALO, GDN_WIDTH),
                                 lambda b, c, col=col: (b + b0, jnp.maximum(cm(c) * per - 1, 0), col)),
                    pl.BlockSpec((1, GDN_HALO, GDN_WIDTH),
                                 lambda b, c, col=col: (b + b0, jnp.minimum((cm(c) + 1) * per, n * per - 1), col))]
        out.append(pl.BlockSpec((1, CHUNK, GDN_GATE_BLOCK), lambda b, c: (b + b0, cm(c), gate_block)))
        return out

    def out_spec(rev):
        return pl.BlockSpec((1, CHUNK, GDN_WIDTH), lambda b, c: (b + b0, n - 1 - c if rev else c, 0))

    state_spec = pl.BlockSpec((1, 2, GDN_HEADS, GDN_DK, GDN_DV), lambda b, c: (b, 0, 0, 0, 0))
    into = tuple(a.reshape(nb_total, t, GDN_WIDTH) for a in into)
    o_shape = jax.ShapeDtypeStruct((nb_total, t, GDN_WIDTH), F32)
    o_f, o_b, sfin = pl.pallas_call(
        _gdn_kernel,
        grid=(bsz, n),
        in_specs=specs(False) + specs(True) + [
            pl.BlockSpec((CONV_W, 3, GDN_WIDTH), lambda b, c: (0, 0, 0)),
            pl.BlockSpec((2, GDN_HEADS), lambda b, c: (0, 0)),
            pl.BlockSpec((2, GDN_HEADS), lambda b, c: (0, 0)),
            state_spec] + [pl.BlockSpec(memory_space=pl.ANY)] * len(into),
        out_specs=[out_spec(False), out_spec(True), state_spec],
        out_shape=[o_shape, o_shape, jax.ShapeDtypeStruct((bsz, 2, GDN_HEADS, GDN_DK, GDN_DV), F32)],
        input_output_aliases={24 + k: k for k in range(len(into))},
        scratch_shapes=[pltpu.VMEM((2, GDN_HEADS, GDN_DK, GDN_DV), F32),
                        pltpu.VMEM((6, CHUNK + 2 * GDN_HALO, GDN_WIDTH), F32)],
        compiler_params=_params("parallel", "arbitrary"),
        name="gdn_scan",
    )(*([proj] * 20), conv_w.reshape(CONV_W, 3, GDN_WIDTH), a_log, dt_bias, s0, *into)
    return o_f, o_b, sfin


def _attn_kernel(q_ref, gain_ref, cos_ref, sin_lo_ref, sin_hi_ref, k_ref, v_ref, *rest, rope):
    o_ref = rest[-1]
    k = k_ref[0]
    v = v_ref[0]
    for g in range(ATT_GROUP):
        gs = slice(g * ATT_HD, (g + 1) * ATT_HD)
        q = _rms_rows(q_ref[0, :, gs]) * gain_ref[...]
        if rope:
            q = (q * cos_ref[...] + pltpu.roll(q, AXIS_DIM // 2, 1) * sin_hi_ref[...]
                 + pltpu.roll(q, ATT_HD - AXIS_DIM // 2, 1) * sin_lo_ref[...])
        s = lax.dot_general(q.astype(BF16), k, NT_DIMS, preferred_element_type=F32) * (ATT_HD ** -0.5)
        m = jnp.max(s, axis=-1, keepdims=True)
        p = jnp.exp(s - m)
        l = jnp.sum(p, axis=-1, keepdims=True)
        o = jnp.dot(p.astype(BF16), v, preferred_element_type=F32)
        o_ref[0, :, gs] = o / l


def _rope_tables(t):
    pos = jnp.arange(t, dtype=jnp.int32)
    lane = jnp.arange(ATT_HD, dtype=jnp.int32)
    axis_pos = jnp.where(lane[None, :] < AXIS_DIM, pos[:, None] // GRID_W, pos[:, None] % GRID_W).astype(F32)
    freqs = ROPE_THETA ** (-(lane % (AXIS_DIM // 2)).astype(F32) * 2.0 / AXIS_DIM)
    ang = axis_pos * freqs[None, :]
    second_half = (lane % AXIS_DIM >= AXIS_DIM // 2)[None, :]
    sin = jnp.sin(ang)
    return jnp.cos(ang), jnp.where(second_half, 0.0, -sin), jnp.where(second_half, sin, 0.0)


def attention(proj, b0, bsz, q_gain, k, v, *, rope, into=(), tq=256):
    nb_total, t_q, _ = proj.shape
    t_k = k.shape[1]
    tq = min(tq, t_q)
    gw = ATT_GROUP * ATT_HD
    tables = _rope_tables(t_q)
    table_spec = pl.BlockSpec((tq, ATT_HD), lambda b, h, i: (i, 0))
    return pl.pallas_call(
        functools.partial(_attn_kernel, rope=rope),
        grid=(bsz, ATT_KV_HEADS, t_q // tq),
        in_specs=[pl.BlockSpec((1, tq, gw), lambda b, h, i: (b + b0, i, h)),
                  pl.BlockSpec((1, ATT_HD), lambda b, h, i: (0, 0)),
                  table_spec, table_spec, table_spec,
                  pl.BlockSpec((1, t_k, ATT_HD), lambda b, h, i: (b, 0, h)),
                  pl.BlockSpec((1, t_k, ATT_HD), lambda b, h, i: (b, 0, h))]
        + [pl.BlockSpec(memory_space=pl.ANY)] * len(into),
        out_specs=pl.BlockSpec((1, tq, gw), lambda b, h, i: (b + b0, i, h)),
        out_shape=jax.ShapeDtypeStruct((nb_total, t_q, ATT_WIDTH), F32),
        input_output_aliases={7 + k: k for k in range(len(into))},
        compiler_params=_params("parallel", "parallel", "parallel"),
        name="attention",
    )(proj, q_gain, *tables, k, v, *(a.reshape(nb_total, t_q, ATT_WIDTH) for a in into))


PEER_TB = 512
PEER_ROUTE_TB = 512
PEER_ET = 1024
PEER_IT = PEER_ET // PEER_NKEYS
assert PEER_IT % 8 == 0
PEER_PAIRS = tuple((a, b) for a in range(PEER_TOPK) for b in range(PEER_TOPK) if (a + 1) * (b + 1) <= PEER_TOPK)


def _top_values(x, count):
    vals = []
    for _ in range(count):
        m = jnp.max(x, axis=0, keepdims=True)
        vals.append(m)
        x = jnp.where(x == m, NEG_INF, x)
    return vals


PEER_CAND_ROWS = _round_up(len(PEER_PAIRS), 8)


def _peer_route_kernel(q_ref, keys_ref, p1_ref, e2_ref, tau_ref, cand_scr):
    def top_candidates(rows):
        cand_scr[PEER_CAND_ROWS - 8:, :] = jnp.full((8, cand_scr.shape[1]), NEG_INF, F32)
        for r, row in enumerate(rows):
            cand_scr[r:r + 1, :] = row
        return _top_values(cand_scr[...], PEER_TOPK)

    for h in range(PEER_HEADS):
        e = []
        tops = []
        for p in range(2):
            r0 = (h * 2 + p) * PEER_HALF
            s = lax.dot_general(keys_ref[h, p], q_ref[:, r0:r0 + PEER_HALF], NT_DIMS, precision=HIGHEST,
                                preferred_element_type=F32)
            top = _top_values(s, PEER_TOPK)
            e.append(jnp.exp(s - top[0]))
            tops.append([jnp.exp(t - top[0]) for t in top])
        z = sum(top_candidates([tops[0][a] * tops[1][b] for a, b in PEER_PAIRS]))
        zinv = 1.0 / z
        p1 = e[0] * zinv
        tau = top_candidates([(tops[0][a] * zinv) * tops[1][b] for a, b in PEER_PAIRS])[-1]
        p1_ref[h] = p1
        e2_ref[h] = e[1]
        tau_ref[h] = jnp.broadcast_to(tau, (8, tau.shape[1]))


def _gelu_tanh(x):
    return 0.5 * x * (1.0 + jnp.tanh(np.sqrt(2.0 / np.pi).astype(np.float32) * (x + 0.044715 * (x * x * x))))


def _peer_dense_kernel(h_ref, u_ref, vt_ref, p1_ref, e2_ref, tau_ref, o_ref, coef_scr):
    @pl.when(pl.program_id(1) == 0)
    def _():
        o_ref[...] = jnp.zeros_like(o_ref)

    act = _gelu_tanh(lax.dot_general(u_ref[...], h_ref[...], NT_DIMS, preferred_element_type=F32))
    tb = h_ref.shape[0]
    for ts in range(tb // 128):
        ls = slice(ts * 128, (ts + 1) * 128)
        for i in range(PEER_IT):
            gate = jnp.zeros((PEER_NKEYS, 128), F32)
            for h in range(PEER_HEADS):
                w = e2_ref[h, :, ls] * p1_ref[h, i:i + 1, ls]
                gate = gate + jnp.where(w >= tau_ref[h, 0:1, ls], w, 0.0)
            rs = slice(i * PEER_NKEYS, (i + 1) * PEER_NKEYS)
            coef_scr[rs, ls] = (gate * act[rs, ls]).astype(BF16)
    o_ref[...] += jnp.dot(vt_ref[...], coef_scr[...], preferred_element_type=F32)


def _peer_residual_kernel(x_ref, yt_ref, gain_ref, gate_ref, o_ref):
    o_ref[...] = x_ref[...] + gate_ref[0] * (_rms_rows(yt_ref[...].T) * gain_ref[...])


def peer_residual(x, q, h, keys, u, v_t, gain, gate):
    nseg, seg_len, d = x.shape
    n = nseg * seg_len
    nb = n // PEER_TB
    p1, e2, tau = pl.pallas_call(
        _peer_route_kernel,
        grid=(n // PEER_ROUTE_TB,),
        in_specs=[pl.BlockSpec((PEER_ROUTE_TB, PEER_HEADS * 2 * PEER_HALF), lambda i: (i, 0)),
                  pl.BlockSpec((PEER_HEADS, 2, PEER_NKEYS, PEER_HALF), lambda i: (0, 0, 0, 0))],
        out_specs=[pl.BlockSpec((PEER_HEADS, PEER_NKEYS, PEER_ROUTE_TB), lambda i: (0, 0, i)),
                   pl.BlockSpec((PEER_HEADS, PEER_NKEYS, PEER_ROUTE_TB), lambda i: (0, 0, i)),
                   pl.BlockSpec((PEER_HEADS, 8, PEER_ROUTE_TB), lambda i: (0, 0, i))],
        out_shape=[jax.ShapeDtypeStruct((PEER_HEADS, PEER_NKEYS, n), F32),
                   jax.ShapeDtypeStruct((PEER_HEADS, PEER_NKEYS, n), F32),
                   jax.ShapeDtypeStruct((PEER_HEADS, 8, n), F32)],
        scratch_shapes=[pltpu.VMEM((PEER_CAND_ROWS, PEER_ROUTE_TB), F32)],
        compiler_params=_params("parallel"),
        name="peer_route",
    )(q.reshape(n, -1), keys)
    n_tiles = N_EXPERTS // PEER_ET
    y_t = pl.pallas_call(
        _peer_dense_kernel,
        grid=(nb, n_tiles),
        in_specs=[pl.BlockSpec((PEER_TB, d), lambda i, e: (i, 0)),
                  pl.BlockSpec((PEER_ET, d), lambda i, e: (e, 0)),
                  pl.BlockSpec((d, PEER_ET), lambda i, e: (0, e)),
                  pl.BlockSpec((PEER_HEADS, PEER_IT, PEER_TB), lambda i, e: (0, e, i)),
                  pl.BlockSpec((PEER_HEADS, PEER_NKEYS, PEER_TB), lambda i, e: (0, 0, i)),
                  pl.BlockSpec((PEER_HEADS, 8, PEER_TB), lambda i, e: (0, 0, i))],
        out_specs=pl.BlockSpec((d, PEER_TB), lambda i, e: (0, i)),
        out_shape=jax.ShapeDtypeStruct((d, n), F32),
        scratch_shapes=[pltpu.VMEM((PEER_ET, PEER_TB), BF16)],
        compiler_params=_params("parallel", "arbitrary"),
        name="peer_dense",
    )(h.reshape(n, d), u, v_t, p1, e2, tau)
    tr = 256
    per_seg = seg_len // tr
    return pl.pallas_call(
        _peer_residual_kernel,
        grid=(n // tr,),
        in_specs=[pl.BlockSpec((tr, d), lambda i: (i, 0)),
                  pl.BlockSpec((d, tr), lambda i: (0, i)),
                  pl.BlockSpec((1, d), lambda i: (0, 0)),
                  pl.BlockSpec((1, 1, d), lambda i: (i // per_seg, 0, 0))],
        out_specs=pl.BlockSpec((tr, d), lambda i: (i, 0)),
        out_shape=jax.ShapeDtypeStruct((n, d), F32),
        compiler_params=_params("parallel"),
        name="peer_residual",
    )(x.reshape(n, d), y_t, gain, gate).reshape(nseg, seg_len, d)


def _rmsnorm(x, g):
    return x * lax.rsqrt(jnp.mean(x * x, axis=-1, keepdims=True) + EPS) * g


def _dft_tables(t):
    c = 1 << (int(np.log2(t)) // 2)
    r = t // c
    k = jnp.arange(t, dtype=jnp.int32)

    def angles(rows):
        return ((rows[:, None] * k[None, :]) % t).astype(F32) * (2.0 * np.pi / t)

    hi = angles(jnp.arange(r, dtype=jnp.int32) * c)[:, None, :]
    lo = angles(jnp.arange(c, dtype=jnp.int32))[None, :, :]
    cos = jnp.cos(hi) * jnp.cos(lo) - jnp.sin(hi) * jnp.sin(lo)
    sin = jnp.sin(hi) * jnp.cos(lo) + jnp.cos(hi) * jnp.sin(lo)
    return cos.reshape(t, t), sin.reshape(t, t)


def fnet_mix(xf, n_ctx_seq, ctx_len):
    nseg, seg_len, _ = xf.shape
    cc, sc = _dft_tables(FNET_CH)
    eye = jnp.eye(FNET_GROUPS, dtype=F32)
    w_ch = jnp.concatenate([jnp.kron(eye, cc), jnp.kron(eye, sc)], axis=1)
    z = matmul(xf.reshape(nseg * seg_len, FNET_WIDTH), w_ch).reshape(nseg, seg_len, 2 * FNET_WIDTH)

    def seq_dft(zz, t):
        bsz = zz.shape[0]
        ct, st = _dft_tables(t)
        scale = (t * FNET_CH) ** -0.5
        w_t = jnp.concatenate([ct, -st], axis=1) * scale
        stacked = jnp.concatenate([zz[..., :FNET_WIDTH], zz[..., FNET_WIDTH:]], axis=1)
        rhs = jnp.swapaxes(stacked, 0, 1).reshape(2 * t, bsz * FNET_WIDTH)
        y = matmul(w_t.astype(BF16), rhs.astype(BF16), tm=1024, tn=1024)
        return jnp.swapaxes(y.reshape(t, bsz, FNET_WIDTH), 0, 1)

    y_ctx = seq_dft(z[0].reshape(n_ctx_seq, ctx_len, 2 * FNET_WIDTH), ctx_len).reshape(1, seg_len, FNET_WIDTH)
    y_lat = seq_dft(z[1:], seg_len)
    return jnp.concatenate([y_ctx, y_lat], axis=0)


def _axial_rope(x):
    t = x.shape[1]
    rows = t // GRID_W
    row = jnp.repeat(jnp.arange(rows, dtype=F32), GRID_W)
    col = jnp.tile(jnp.arange(GRID_W, dtype=F32), rows)
    freqs = ROPE_THETA ** (-jnp.arange(AXIS_DIM // 2, dtype=F32) * 2.0 / AXIS_DIM)

    def rot(xa, pos):
        ang = pos[:, None] * freqs[None, :]
        cos, sin = jnp.cos(ang)[None, :, None, :], jnp.sin(ang)[None, :, None, :]
        x1, x2 = xa[..., :AXIS_DIM // 2], xa[..., AXIS_DIM // 2:]
        return jnp.concatenate([x1 * cos - x2 * sin, x2 * cos + x1 * sin], axis=-1)

    return jnp.concatenate([rot(x[..., :AXIS_DIM], row), rot(x[..., AXIS_DIM:], col)], axis=-1)


EVEN_COLS = (GLA_HEADS * GLA_DK, GLA_HEADS * GLA_DK, GLA_WIDTH, GLA_WIDTH, FNET_WIDTH, GLA_GATE_RANK, GLA_GATE_RANK)
ODD_COLS = (ATT_WIDTH, GDN_WIDTH, GDN_WIDTH, GDN_WIDTH, GDN_WIDTH, KV_WIDTH, KV_WIDTH,
            GDN_HEADS, GDN_HEADS, GDN_HEADS, GDN_HEADS)
PROJ_TN = 768


def _pad_cols(w):
    return jnp.pad(w, ((0, 0), (0, _round_up(w.shape[1], PROJ_TN) - w.shape[1])))


def _even_weight(w_in):
    q, k, v, g, lr_f, lr_b, xf = _split_cols(w_in, EVEN_SPLITS)
    return _pad_cols(jnp.concatenate([q, k, v, g, xf, lr_f, lr_b], axis=1)).astype(BF16)


def _odd_weight(w_in):
    qa, ka, va, qkv, z, b_f, b_b, a_f, a_b = _split_cols(w_in, ODD_SPLITS)
    return _pad_cols(jnp.concatenate([qa, qkv, z, ka, va, b_f, b_b, a_f, a_b], axis=1)).astype(BF16)


def _even_mix(proj, n_ctx, ctx_len, gate_w2, gate_b, state):
    nseg, seg_len, width = proj.shape
    xf = proj[..., sum(EVEN_COLS[:4]):sum(EVEN_COLS[:5])]
    zeros = jnp.zeros_like(gate_w2[0])
    w2 = jnp.stack([jnp.concatenate([gate_w2[0], zeros]), jnp.concatenate([zeros, gate_w2[1]])])
    gb = gate_b[:, None, :]
    zero = jnp.zeros((n_ctx, 2, GLA_HEADS, GLA_DK, GLA_DV), F32)
    of_c, ob_c, s_ctx = gla_scan(proj.reshape(nseg * n_ctx, ctx_len, width), 0, n_ctx, w2, gb, zero)
    o_f, o_b, _ = gla_scan(proj, 1, nseg - 1, w2, gb, state, into=(of_c, ob_c))
    return o_f, o_b, fnet_mix(xf, n_ctx, ctx_len), s_ctx


def _odd_mix(proj, n_ctx, ctx_len, qk_g, conv_w, a_log, dt_bias, ctx_k, ctx_v, state):
    nseg, seg_len, width = proj.shape
    ka, va = _split_cols(proj[..., sum(ODD_COLS[:5]):sum(ODD_COLS[:7])], ODD_COLS[5:7])
    proj_ctx = proj.reshape(nseg * n_ctx, ctx_len, width)
    q_gain = qk_g[0][None, :]

    def keys_values(sl, bsz, t, latent):
        k4 = _rmsnorm(ka[sl].reshape(bsz, t, ATT_KV_HEADS, ATT_HD), qk_g[1])
        v4 = va[sl].reshape(bsz, t, ATT_KV_HEADS, ATT_HD)
        if latent:
            k_all = jnp.concatenate([ctx_k, _axial_rope(k4)], axis=1)
            v_all = jnp.concatenate([ctx_v, v4], axis=1)
        else:
            k_all, v_all = k4, v4
        t_k = k_all.shape[1]
        return (k_all.reshape(bsz, t_k, KV_WIDTH).astype(BF16), v_all.reshape(bsz, t_k, KV_WIDTH).astype(BF16),
                k4, v4)

    kc, vc, k_ctx, v_ctx = keys_values(slice(0, 1), n_ctx, ctx_len, False)
    kl, vl, _, _ = keys_values(slice(1, None), nseg - 1, seg_len, True)
    att_c = attention(proj_ctx, 0, n_ctx, q_gain, kc, vc, rope=False)
    att = attention(proj, 1, nseg - 1, q_gain, kl, vl, rope=True, into=(att_c,))
    zero = jnp.zeros((n_ctx, 2, GDN_HEADS, GDN_DK, GDN_DV), F32)
    of_c, ob_c, s_ctx = gdn_scan(proj_ctx, 0, n_ctx, conv_w, a_log, dt_bias, zero)
    o_f, o_b, _ = gdn_scan(proj, 1, nseg - 1, conv_w, a_log, dt_bias, state, into=(of_c, ob_c))
    return o_f, o_b, att, k_ctx, v_ctx, s_ctx


def kernel(x_prompt, x_sample, cache_k, cache_v, state_gla, state_delta, c, c_ctx, ada_w, ada_b, norm_g, w_out, even_w_in, gla_gate_w2, gla_gate_b, gla_out_norm, odd_w_in, qk_norm, conv_w, gdn_a_log, gdn_dt_bias, gdn_out_norm, peer_wq, peer_keys, peer_u, peer_v):
    n_ctx, ctx_len, d = x_prompt.shape
    n_lat, seg_len, _ = x_sample.shape
    assert n_ctx * ctx_len == seg_len and d == D_MODEL
    depth = ada_w.shape[0]
    x = jnp.concatenate([x_prompt.reshape(1, seg_len, d), x_sample], axis=0)
    cvec = jnp.concatenate([c_ctx[None, :], c], axis=0)
    new_gla, new_delta, new_k, new_v = [], [], [], []
    for l in range(depth):
        i = l // 2
        mod = matmul(jax.nn.silu(cvec), ada_w[l], tn=1024) + ada_b[l]
        sh1, sc1, g1, sh2, sc2, g2 = [m[:, None, :] for m in jnp.split(mod, 6, axis=-1)]
        gains = norm_g[l][:, None, :]
        w_o = w_out[l].astype(BF16)
        if l % 2 == 0:
            proj = norm_matmul(x, gains[0], 1 + sc1, sh1, _even_weight(even_w_in[i]), tm=1024, tn=PROJ_TN)
            o_f, o_b, other, s_new = _even_mix(proj, n_ctx, ctx_len, gla_gate_w2[i], gla_gate_b[i], state_gla[:, i])
            new_gla.append(s_new)
            x = proj_residual(o_f, o_b, proj, gla_out_norm[i][None, :], other, w_o, x, gains[1], g1,
                              heads=GLA_HEADS, hd=GLA_DV, z_block=2, normed_first=True)
        else:
            proj = norm_matmul(x, gains[0], 1 + sc1, sh1, _odd_weight(odd_w_in[i]), tm=1024, tn=PROJ_TN)
            o_f, o_b, other, k_new, v_new, s_new = _odd_mix(proj, n_ctx, ctx_len, qk_norm[i], conv_w[i], gdn_a_log[i],
                                                            gdn_dt_bias[i], cache_k[:, i], cache_v[:, i],
                                                            state_delta[:, i])
            new_k.append(k_new)
            new_v.append(v_new)
            new_delta.append(s_new)
            x = proj_residual(o_f, o_b, proj, gdn_out_norm[i][None, :], other, w_o, x, gains[1], g1,
                              heads=GDN_HEADS, hd=GDN_DV, z_block=4, normed_first=False)
        q, h = norm_matmul(x, gains[2], 1 + sc2, sh2, peer_wq[l].astype(BF16), tm=512, tn=1024, emit_h=True)
        x = peer_residual(x, q, h, peer_keys[l], peer_u[l].astype(BF16), peer_v[l].T.astype(BF16), gains[3], g2)
    return (x[0].reshape(n_ctx, ctx_len, d), x[1:],
            jnp.stack(new_k, axis=1), jnp.stack(new_v, axis=1),
            jnp.stack(new_gla, axis=1), jnp.stack(new_delta, axis=1))
```

```python
import functools

import numpy as np
import jax
import jax.numpy as jnp
from jax import lax
from jax.experimental import pallas as pl
from jax.experimental.pallas import tpu as pltpu

F32 = jnp.float32
BF16 = jnp.bfloat16
HIGHEST = lax.Precision.HIGHEST

D_MODEL = 2048
EPS = 1e-6
CHUNK = 64

GLA_HEADS = 6
GLA_DK = 128
GLA_DV = 256
GLA_WIDTH = GLA_HEADS * GLA_DV
GLA_GATE_RANK = 16
GLA_GATE_NORM = 16.0
FNET_WIDTH = D_MODEL - GLA_WIDTH
FNET_GROUPS = 4
FNET_CH = FNET_WIDTH // FNET_GROUPS

ATT_HD = 128
ATT_HEADS = 8
ATT_KV_HEADS = 2
ATT_GROUP = ATT_HEADS // ATT_KV_HEADS
ATT_WIDTH = ATT_HEADS * ATT_HD
KV_WIDTH = ATT_KV_HEADS * ATT_HD
AXIS_DIM = ATT_HD // 2
ROPE_THETA = 10000.0
GRID_W = 64
GDN_HEADS = 8
GDN_DK = 128
GDN_DV = 128
GDN_WIDTH = GDN_HEADS * GDN_DV
CONV_W = 3

PEER_HEADS = 8
PEER_NKEYS = 128
PEER_HALF = 128
PEER_TOPK = 16
N_EXPERTS = PEER_NKEYS ** 2

EVEN_SPLITS = (GLA_HEADS * GLA_DK, GLA_HEADS * GLA_DK, GLA_WIDTH, GLA_WIDTH, GLA_GATE_RANK, GLA_GATE_RANK, FNET_WIDTH)
ODD_SPLITS = (ATT_WIDTH, KV_WIDTH, KV_WIDTH, 3 * GDN_WIDTH, GDN_WIDTH, GDN_HEADS, GDN_HEADS, GDN_HEADS, GDN_HEADS)

VMEM_LIMIT_BYTES = 56 * 1024 * 1024
NEG_INF = float("-inf")

NT_DIMS = (((1,), (1,)), ((), ()))


def _params(*semantics):
    return pltpu.CompilerParams(dimension_semantics=semantics, vmem_limit_bytes=VMEM_LIMIT_BYTES)


def _split_cols(x, sizes):
    return jnp.split(x, [int(s) for s in np.cumsum(sizes)[:-1]], axis=-1)


def _round_up(n, m):
    return -(-n // m) * m


def _mm_kernel(x_ref, w_ref, o_ref, acc_ref):
    k = pl.program_id(2)

    @pl.when(k == 0)
    def _():
        acc_ref[...] = jnp.zeros_like(acc_ref)

    acc_ref[...] += jnp.dot(x_ref[...].astype(BF16), w_ref[...].astype(BF16), preferred_element_type=F32)

    @pl.when(k == pl.num_programs(2) - 1)
    def _():
        o_ref[...] = acc_ref[...]


def _pick_tile(n, pref):
    for t in (pref, 1024, 512, 256, 128):
        if t <= pref and n % t == 0:
            return t
    return n


def matmul(x, w, *, tm=1024, tn=512, tk=2048):
    m, kdim = x.shape
    n = w.shape[1]
    m_pad = _round_up(m, 8)
    n_pad = _round_up(n, 256)
    if m_pad != m:
        x = jnp.pad(x, ((0, m_pad - m), (0, 0)))
    if n_pad != n:
        w = jnp.pad(w, ((0, 0), (0, n_pad - n)))
    tm = _pick_tile(m_pad, tm)
    tn = _pick_tile(n_pad, tn)
    tk = _pick_tile(kdim, tk)
    out = pl.pallas_call(
        _mm_kernel,
        grid=(m_pad // tm, n_pad // tn, kdim // tk),
        in_specs=[pl.BlockSpec((tm, tk), lambda i, j, k: (i, k)),
                  pl.BlockSpec((tk, tn), lambda i, j, k: (k, j))],
        out_specs=pl.BlockSpec((tm, tn), lambda i, j, k: (i, j)),
        out_shape=jax.ShapeDtypeStruct((m_pad, n_pad), F32),
        scratch_shapes=[pltpu.VMEM((tm, tn), F32)],
        compiler_params=_params("parallel", "parallel", "arbitrary"),
        name="matmul",
    )(x, w)
    return out[:m, :n]


def _rms_rows(x):
    return x * lax.rsqrt(jnp.mean(x * x, axis=-1, keepdims=True) + EPS)


def _l2_rows(x):
    return x * lax.rsqrt(jnp.sum(x * x, axis=-1, keepdims=True) + EPS)


def _norm_matmul_kernel(x_ref, gain_ref, scale_ref, shift_ref, w_ref, *rest, emit_h):
    if emit_h:
        o_ref, h_ref, h_scr = rest
    else:
        o_ref, h_scr = rest

    @pl.when(pl.program_id(2) == 0)
    def _():
        h = _rms_rows(x_ref[0]) * (gain_ref[...] * scale_ref[0]) + shift_ref[0]
        h_scr[...] = h.astype(BF16)
        if emit_h:
            h_ref[0] = h_scr[...]

    o_ref[0] = jnp.dot(h_scr[...], w_ref[...], preferred_element_type=F32)


def norm_matmul(x, gain, scale, shift, w, *, tm, tn, emit_h=False):
    nseg, seg_len, d = x.shape
    n = w.shape[1]
    tm = min(tm, seg_len)
    out_shape = [jax.ShapeDtypeStruct((nseg, seg_len, n), F32)]
    out_specs = [pl.BlockSpec((1, tm, tn), lambda s, i, j: (s, i, j))]
    if emit_h:
        out_shape.append(jax.ShapeDtypeStruct((nseg, seg_len, d), BF16))
        out_specs.append(pl.BlockSpec((1, tm, d), lambda s, i, j: (s, i, 0)))
    res = pl.pallas_call(
        functools.partial(_norm_matmul_kernel, emit_h=emit_h),
        grid=(nseg, seg_len // tm, n // tn),
        in_specs=[pl.BlockSpec((1, tm, d), lambda s, i, j: (s, i, 0)),
                  pl.BlockSpec((1, d), lambda s, i, j: (0, 0)),
                  pl.BlockSpec((1, 1, d), lambda s, i, j: (s, 0, 0)),
                  pl.BlockSpec((1, 1, d), lambda s, i, j: (s, 0, 0)),
                  pl.BlockSpec((d, tn), lambda s, i, j: (0, j))],
        out_specs=out_specs,
        out_shape=out_shape,
        scratch_shapes=[pltpu.VMEM((tm, d), BF16)],
        compiler_params=_params("parallel", "parallel", "arbitrary"),
        name="norm_matmul",
    )(x, gain, scale, shift, w)
    return res if emit_h else res[0]


def _proj_residual_kernel(of_ref, ob_ref, z_ref, hg_ref, b_ref, w_ref, x_ref, gain_ref, gate_ref, o_ref,
                          *, heads, hd, normed_first):
    ka = heads * hd
    o = of_ref[0] + ob_ref[0]
    z = z_ref[0]
    parts = []
    for h in range(heads):
        hs = slice(h * hd, (h + 1) * hd)
        zh = z[:, hs]
        parts.append((_rms_rows(o[:, hs]) * hg_ref[...] * (zh * jax.nn.sigmoid(zh))).astype(BF16))
    a = jnp.concatenate(parts, axis=-1)
    kb = w_ref.shape[0] - ka
    wa, wb = (w_ref[:ka, :], w_ref[ka:, :]) if normed_first else (w_ref[kb:, :], w_ref[:kb, :])
    y = jnp.dot(a, wa, preferred_element_type=F32) + jnp.dot(b_ref[0].astype(BF16), wb, preferred_element_type=F32)
    o_ref[0] = x_ref[0] + gate_ref[0] * (_rms_rows(y) * gain_ref[...])


def proj_residual(o_f, o_b, z, head_gain, b, w, x, gain, gate, *, heads, hd, z_block, normed_first, tm=256):
    nseg, seg_len, d = x.shape
    ka = heads * hd
    kb = d - ka

    def row_spec(width, col=0):
        return pl.BlockSpec((1, tm, width), lambda s, i: (s, i, col))

    return pl.pallas_call(
        functools.partial(_proj_residual_kernel, heads=heads, hd=hd, normed_first=normed_first),
        grid=(nseg, seg_len // tm),
        in_specs=[row_spec(ka), row_spec(ka), row_spec(ka, z_block),
                  pl.BlockSpec((1, hd), lambda s, i: (0, 0)),
                  row_spec(kb),
                  pl.BlockSpec((d, d), lambda s, i: (0, 0)),
                  row_spec(d),
                  pl.BlockSpec((1, d), lambda s, i: (0, 0)),
                  pl.BlockSpec((1, 1, d), lambda s, i: (s, 0, 0))],
        out_specs=row_spec(d),
        out_shape=jax.ShapeDtypeStruct((nseg, seg_len, d), F32),
        compiler_params=_params("parallel", "parallel"),
        name="proj_residual",
    )(o_f, o_b, z, head_gain, b, w, x, gain, gate)


def _split_hi_lo(x):
    hi = x.astype(BF16)
    lo = (x - hi.astype(F32)).astype(BF16)
    return hi, lo


GLA_LR_BLOCK = 128


def _gla_kernel(qf_ref, kf_ref, vf_ref, lrf_ref, qb_ref, kb_ref, vb_ref, lrb_ref, w2_ref, gb_ref, s0_ref, *rest):
    of_ref, ob_ref, sfin_ref, st_scr = rest[-4:]
    c = pl.program_id(1)

    @pl.when(c == 0)
    def _():
        st_scr[...] = s0_ref[0]

    ii = lax.broadcasted_iota(jnp.int32, (CHUNK, CHUNK), 0)
    jj = lax.broadcasted_iota(jnp.int32, (CHUNK, CHUNK), 1)
    mid = CHUNK // 2
    jobs = []
    for d, (q_ref, k_ref, v_ref, lr_ref, o_ref) in enumerate(((qf_ref, kf_ref, vf_ref, lrf_ref, of_ref),
                                                              (qb_ref, kb_ref, vb_ref, lrb_ref, ob_ref))):
        incl = (ii <= jj) if d else (jj <= ii)
        tri = jnp.where(incl, 1.0, 0.0).astype(BF16)
        lr = lr_ref[0, :, :2 * GLA_GATE_RANK].astype(BF16)
        pre = jnp.dot(lr, w2_ref[d].astype(BF16), preferred_element_type=F32) + gb_ref[d]
        logd = (jnp.minimum(pre, 0.0) - jnp.log1p(jnp.exp(-jnp.abs(pre)))) / GLA_GATE_NORM
        g_hi, g_lo = _split_hi_lo(logd)
        b = jnp.dot(tri, g_hi, preferred_element_type=F32) + jnp.dot(tri, g_lo, preferred_element_type=F32)
        b_mid = b[mid - 1:mid] if d else b[mid:mid + 1]
        b_last = b[0:1] if d else b[CHUNK - 1:CHUNK]
        e_q = jnp.exp(b - b_mid)
        e_k = jnp.exp(b_mid - b)
        e_b = jnp.exp(b)
        e_kl = jnp.exp(b_last - b)
        e_l = jnp.exp(b_last)
        for h in range(GLA_HEADS):
            ks = slice(h * GLA_DK, (h + 1) * GLA_DK)
            vs = slice(h * GLA_DV, (h + 1) * GLA_DV)
            qi = q_ref[0, :, ks] * (GLA_DK ** -0.5)
            ki = k_ref[0, :, ks]
            jobs.append(dict(d=d, h=h, vs=vs, o_ref=o_ref, incl=incl,
                             qe=(qi * e_q[:, ks]).astype(BF16), ke=(ki * e_k[:, ks]).astype(BF16),
                             qb=(qi * e_b[:, ks]).astype(BF16), kd=(ki * e_kl[:, ks]).astype(BF16),
                             vi=v_ref[0, :, vs].astype(BF16), e_l=e_l[:, ks]))
    for j in jobs:
        a = lax.dot_general(j["qe"], j["ke"], NT_DIMS, preferred_element_type=F32)
        j["a"] = jnp.where(j["incl"], a, 0.0).astype(BF16)
    for j in jobs:
        j["st"] = st_scr[j["d"], j["h"]]
        j["o"] = lax.dot_general(j["qb"], j["st"].astype(BF16), NT_DIMS, preferred_element_type=F32)
    for j in jobs:
        j["o"] = j["o"] + jnp.dot(j["a"], j["vi"], preferred_element_type=F32)
    for j in jobs:
        j["st_new"] = j["st"] * j["e_l"] + jnp.dot(j["vi"].T, j["kd"], preferred_element_type=F32)
    for j in jobs:
        j["o_ref"][0, :, j["vs"]] = j["o"]
        st_scr[j["d"], j["h"]] = j["st_new"]

    @pl.when(c == pl.num_programs(1) - 1)
    def _():
        sfin_ref[0] = st_scr[...]


def gla_scan(proj, b0, bsz, w2, gb, s0, into=()):
    nb_total, t, _ = proj.shape
    n = t // CHUNK
    s0t = jnp.swapaxes(s0, -1, -2)
    kw = GLA_HEADS * GLA_DK
    lr_block = (2 * kw + 2 * GLA_WIDTH + FNET_WIDTH) // GLA_LR_BLOCK

    def specs(rev):
        def cm(c):
            return n - 1 - c if rev else c
        return [pl.BlockSpec((1, CHUNK, kw), lambda b, c: (b + b0, cm(c), 0)),
                pl.BlockSpec((1, CHUNK, kw), lambda b, c: (b + b0, cm(c), 1)),
                pl.BlockSpec((1, CHUNK, GLA_WIDTH), lambda b, c: (b + b0, cm(c), 1)),
                pl.BlockSpec((1, CHUNK, GLA_LR_BLOCK), lambda b, c: (b + b0, cm(c), lr_block))]

    def out_spec(rev):
        return pl.BlockSpec((1, CHUNK, GLA_WIDTH), lambda b, c: (b + b0, n - 1 - c if rev else c, 0))

    state_spec = pl.BlockSpec((1, 2, GLA_HEADS, GLA_DV, GLA_DK), lambda b, c: (b, 0, 0, 0, 0))
    into = tuple(a.reshape(nb_total, t, GLA_WIDTH) for a in into)
    o_shape = jax.ShapeDtypeStruct((nb_total, t, GLA_WIDTH), F32)
    o_f, o_b, sfin = pl.pallas_call(
        _gla_kernel,
        grid=(bsz, n),
        in_specs=specs(False) + specs(True) + [
            pl.BlockSpec((2, 2 * GLA_GATE_RANK, kw), lambda b, c: (0, 0, 0)),
            pl.BlockSpec((2, 1, kw), lambda b, c: (0, 0, 0)),
            state_spec] + [pl.BlockSpec(memory_space=pl.ANY)] * len(into),
        out_specs=[out_spec(False), out_spec(True), state_spec],
        out_shape=[o_shape, o_shape, jax.ShapeDtypeStruct((bsz, 2, GLA_HEADS, GLA_DV, GLA_DK), F32)],
        input_output_aliases={11 + k: k for k in range(len(into))},
        scratch_shapes=[pltpu.VMEM((2, GLA_HEADS, GLA_DV, GLA_DK), F32)],
        compiler_params=_params("parallel", "arbitrary"),
        name="gla_scan",
    )(*([proj] * 8), w2, gb, s0t, *into)
    return o_f, o_b, jnp.swapaxes(sfin, -1, -2)


GDN_HALO = 8
GDN_GATE_BLOCK = 128


def _gdn_kernel(*refs):
    dir_refs = (refs[:10], refs[10:20])
    cw_ref, alog_ref, dtb_ref, s0_ref = refs[20:24]
    of_ref, ob_ref, sfin_ref, s_scr, win_scr = refs[-5:]
    c = pl.program_id(1)
    n = pl.num_programs(1)

    @pl.when(c == 0)
    def _():
        s_scr[...] = s0_ref[0]

    ii = lax.broadcasted_iota(jnp.int32, (CHUNK, CHUNK), 0)
    jj = lax.broadcasted_iota(jnp.int32, (CHUNK, CHUNK), 1)
    eye = jnp.where(ii == jj, 1.0, 0.0).astype(F32)

    jobs = []
    for d, o_ref in enumerate((of_ref, ob_ref)):
        chunk = n - 1 - c if d else c
        has_prev = (chunk > 0).astype(F32)
        has_next = (chunk < n - 1).astype(F32)
        pieces = []
        for p in range(3):
            cur_ref, prev_ref, next_ref = dir_refs[d][3 * p:3 * p + 3]
            win = win_scr.at[3 * d + p]
            win[GDN_HALO:GDN_HALO + CHUNK, :] = cur_ref[0]
            win[GDN_HALO - 1:GDN_HALO, :] = prev_ref[0, GDN_HALO - 1:GDN_HALO, :] * has_prev
            win[GDN_HALO + CHUNK:GDN_HALO + CHUNK + 1, :] = next_ref[0, 0:1, :] * has_next
            y = (win[GDN_HALO - 1:GDN_HALO - 1 + CHUNK, :] * cw_ref[0, p:p + 1, :]
                 + win[GDN_HALO:GDN_HALO + CHUNK, :] * cw_ref[1, p:p + 1, :]
                 + win[GDN_HALO + 1:GDN_HALO + 1 + CHUNK, :] * cw_ref[2, p:p + 1, :])
            pieces.append(y * jax.nn.sigmoid(y))
        gates = dir_refs[d][9][0]
        beta = jax.nn.sigmoid(gates[:, GDN_HEADS * d:GDN_HEADS * (d + 1)])
        ga = gates[:, GDN_HEADS * (2 + d):GDN_HEADS * (3 + d)] + dtb_ref[d:d + 1, :]
        softplus = jnp.maximum(ga, 0.0) + jnp.log1p(jnp.exp(-jnp.abs(ga)))
        logd = -jnp.exp(alog_ref[d:d + 1, :]) * softplus
        order = ii - jj if d else jj - ii
        incl = order <= 0
        strict = order < 0
        tri = jnp.where(incl, 1.0, 0.0).astype(BF16)
        g_hi, g_lo = _split_hi_lo(logd)
        dcol = jnp.dot(tri, g_hi, preferred_element_type=F32) + jnp.dot(tri, g_lo, preferred_element_type=F32)
        logd_t = jnp.concatenate([logd, jnp.zeros((CHUNK, 128 - GDN_HEADS), F32)], axis=1).T[:GDN_HEADS]
        gt_hi, gt_lo = _split_hi_lo(logd_t)
        drow = (lax.dot_general(gt_hi, tri, NT_DIMS, preferred_element_type=F32)
                + lax.dot_general(gt_lo, tri, NT_DIMS, preferred_element_type=F32))
        last = 0 if d else CHUNK - 1
        for h in range(GDN_HEADS):
            hs = slice(h * GDN_DK, (h + 1) * GDN_DK)
            qi = _l2_rows(pieces[0][:, hs]) * (GDN_DK ** -0.5)
            ki = _l2_rows(pieces[1][:, hs])
            dc = dcol[:, h:h + 1]
            bc = beta[:, h:h + 1]
            decay = jnp.exp(jnp.where(incl, dc - drow[h:h + 1, :], NEG_INF))
            kb = ki * bc
            kq = jnp.concatenate([kb, qi], axis=0).astype(BF16)
            kqk = lax.dot_general(kq, ki.astype(BF16), NT_DIMS, preferred_element_type=F32)
            a = jnp.where(strict, kqk[:CHUNK] * decay, 0.0)
            e_d = jnp.exp(dc)
            d_last = dc[last:last + 1]
            jobs.append(dict(
                d=d, h=h, hs=hs, o_ref=o_ref, p=a, tinv=eye - a, att=(kqk[CHUNK:] * decay).astype(BF16),
                rhs=jnp.concatenate([pieces[2][:, hs] * bc, kb * e_d], axis=-1).astype(BF16),
                qe=qi * e_d, kd=(ki * jnp.exp(d_last - dc)).astype(BF16), s_decay=jnp.exp(d_last)))

    for _ in range(5):
        for j in jobs:
            pb = j["p"].astype(BF16)
            j["p"] = jnp.dot(pb, pb, preferred_element_type=F32)
        for j in jobs:
            j["tinv"] = j["tinv"] + jnp.dot(j["tinv"].astype(BF16), j["p"].astype(BF16), preferred_element_type=F32)
    for j in jobs:
        j["sol"] = jnp.dot(j["tinv"].astype(BF16), j["rhs"], preferred_element_type=F32)
    for j in jobs:
        j["s"] = s_scr[j["d"], j["h"]]
        wq = jnp.concatenate([j["sol"][:, GDN_DV:], j["qe"]], axis=0).astype(BF16)
        j["ws"] = jnp.dot(wq, j["s"].astype(BF16), preferred_element_type=F32)
    for j in jobs:
        j["vnb"] = (j["sol"][:, :GDN_DV] - j["ws"][:CHUNK]).astype(BF16)
        j["o"] = j["ws"][CHUNK:] + jnp.dot(j["att"], j["vnb"], preferred_element_type=F32)
    for j in jobs:
        j["s_new"] = j["s_decay"] * j["s"] + jnp.dot(j["kd"].T, j["vnb"], preferred_element_type=F32)
    for j in jobs:
        j["o_ref"][0, :, j["hs"]] = j["o"]
        s_scr[j["d"], j["h"]] = j["s_new"]

    @pl.when(c == pl.num_programs(1) - 1)
    def _():
        sfin_ref[0] = s_scr[...]


def gdn_scan(proj, b0, bsz, conv_w, a_log, dt_bias, s0, into=()):
    nb_total, t, _ = proj.shape
    n = t // CHUNK
    per = CHUNK // GDN_HALO
    gate_block = (ATT_WIDTH + 4 * GDN_WIDTH + 2 * KV_WIDTH) // GDN_GATE_BLOCK

    def specs(rev):
        def cm(c):
            return n - 1 - c if rev else c
        out = []
        for p in range(3):
            col = 1 + p
            out += [pl.BlockSpec((1, CHUNK, GDN_WIDTH), lambda b, c, col=col: (b + b0, cm(c), col)),
                    pl.BlockSpec((1, GDN_HALO, GDN_WIDTH),
                                 lambda b, c, col=col: (b + b0, jnp.maximum(cm(c) * per - 1, 0), col)),
                    pl.BlockSpec((1, GDN_HALO, GDN_WIDTH),
                                 lambda b, c, col=col: (b + b0, jnp.minimum((cm(c) + 1) * per, n * per - 1), col))]
        out.append(pl.BlockSpec((1, CHUNK, GDN_GATE_BLOCK), lambda b, c: (b + b0, cm(c), gate_block)))
        return out

    def out_spec(rev):
        return pl.BlockSpec((1, CHUNK, GDN_WIDTH), lambda b, c: (b + b0, n - 1 - c if rev else c, 0))

    state_spec = pl.BlockSpec((1, 2, GDN_HEADS, GDN_DK, GDN_DV), lambda b, c: (b, 0, 0, 0, 0))
    into = tuple(a.reshape(nb_total, t, GDN_WIDTH) for a in into)
    o_shape = jax.ShapeDtypeStruct((nb_total, t, GDN_WIDTH), F32)
    o_f, o_b, sfin = pl.pallas_call(
        _gdn_kernel,
        grid=(bsz, n),
        in_specs=specs(False) + specs(True) + [
            pl.BlockSpec((CONV_W, 3, GDN_WIDTH), lambda b, c: (0, 0, 0)),
            pl.BlockSpec((2, GDN_HEADS), lambda b, c: (0, 0)),
            pl.BlockSpec((2, GDN_HEADS), lambda b, c: (0, 0)),
            state_spec] + [pl.BlockSpec(memory_space=pl.ANY)] * len(into),
        out_specs=[out_spec(False), out_spec(True), state_spec],
        out_shape=[o_shape, o_shape, jax.ShapeDtypeStruct((bsz, 2, GDN_HEADS, GDN_DK, GDN_DV), F32)],
        input_output_aliases={24 + k: k for k in range(len(into))},
        scratch_shapes=[pltpu.VMEM((2, GDN_HEADS, GDN_DK, GDN_DV), F32),
                        pltpu.VMEM((6, CHUNK + 2 * GDN_HALO, GDN_WIDTH), F32)],
        compiler_params=_params("parallel", "arbitrary"),
        name="gdn_scan",
    )(*([proj] * 20), conv_w.reshape(CONV_W, 3, GDN_WIDTH), a_log, dt_bias, s0, *into)
    return o_f, o_b, sfin


def _attn_kernel(q_ref, gain_ref, cos_ref, sin_lo_ref, sin_hi_ref, k_ref, v_ref, *rest, rope):
    o_ref = rest[-1]
    k = k_ref[0]
    v = v_ref[0]
    for g in range(ATT_GROUP):
        gs = slice(g * ATT_HD, (g + 1) * ATT_HD)
        q = _rms_rows(q_ref[0, :, gs]) * gain_ref[...]
        if rope:
            q = (q * cos_ref[...] + pltpu.roll(q, AXIS_DIM // 2, 1) * sin_hi_ref[...]
                 + pltpu.roll(q, ATT_HD - AXIS_DIM // 2, 1) * sin_lo_ref[...])
        s = lax.dot_general(q.astype(BF16), k, NT_DIMS, preferred_element_type=F32) * (ATT_HD ** -0.5)
        m = jnp.max(s, axis=-1, keepdims=True)
        p = jnp.exp(s - m)
        l = jnp.sum(p, axis=-1, keepdims=True)
        o = jnp.dot(p.astype(BF16), v, preferred_element_type=F32)
        o_ref[0, :, gs] = o / l


def _rope_tables(t):
    pos = jnp.arange(t, dtype=jnp.int32)
    lane = jnp.arange(ATT_HD, dtype=jnp.int32)
    axis_pos = jnp.where(lane[None, :] < AXIS_DIM, pos[:, None] // GRID_W, pos[:, None] % GRID_W).astype(F32)
    freqs = ROPE_THETA ** (-(lane % (AXIS_DIM // 2)).astype(F32) * 2.0 / AXIS_DIM)
    ang = axis_pos * freqs[None, :]
    second_half = (lane % AXIS_DIM >= AXIS_DIM // 2)[None, :]
    sin = jnp.sin(ang)
    return jnp.cos(ang), jnp.where(second_half, 0.0, -sin), jnp.where(second_half, sin, 0.0)


def attention(proj, b0, bsz, q_gain, k, v, *, rope, into=(), tq=256):
    nb_total, t_q, _ = proj.shape
    t_k = k.shape[1]
    tq = min(tq, t_q)
    gw = ATT_GROUP * ATT_HD
    tables = _rope_tables(t_q)
    table_spec = pl.BlockSpec((tq, ATT_HD), lambda b, h, i: (i, 0))
    return pl.pallas_call(
        functools.partial(_attn_kernel, rope=rope),
        grid=(bsz, ATT_KV_HEADS, t_q // tq),
        in_specs=[pl.BlockSpec((1, tq, gw), lambda b, h, i: (b + b0, i, h)),
                  pl.BlockSpec((1, ATT_HD), lambda b, h, i: (0, 0)),
                  table_spec, table_spec, table_spec,
                  pl.BlockSpec((1, t_k, ATT_HD), lambda b, h, i: (b, 0, h)),
                  pl.BlockSpec((1, t_k, ATT_HD), lambda b, h, i: (b, 0, h))]
        + [pl.BlockSpec(memory_space=pl.ANY)] * len(into),
        out_specs=pl.BlockSpec((1, tq, gw), lambda b, h, i: (b + b0, i, h)),
        out_shape=jax.ShapeDtypeStruct((nb_total, t_q, ATT_WIDTH), F32),
        input_output_aliases={7 + k: k for k in range(len(into))},
        compiler_params=_params("parallel", "parallel", "parallel"),
        name="attention",
    )(proj, q_gain, *tables, k, v, *(a.reshape(nb_total, t_q, ATT_WIDTH) for a in into))


PEER_TB = 512
PEER_ROUTE_TB = 512
PEER_ET = 1024
PEER_IT = PEER_ET // PEER_NKEYS
assert PEER_IT % 8 == 0
PEER_PAIRS = tuple((a, b) for a in range(PEER_TOPK) for b in range(PEER_TOPK) if (a + 1) * (b + 1) <= PEER_TOPK)


def _top_values(x, count):
    vals = []
    for _ in range(count):
        m = jnp.max(x, axis=0, keepdims=True)
        vals.append(m)
        x = jnp.where(x == m, NEG_INF, x)
    return vals


PEER_CAND_ROWS = _round_up(len(PEER_PAIRS), 8)


def _peer_route_kernel(q_ref, keys_ref, p1_ref, e2_ref, tau_ref, cand_scr):
    def top_candidates(rows):
        cand_scr[PEER_CAND_ROWS - 8:, :] = jnp.full((8, cand_scr.shape[1]), NEG_INF, F32)
        for r, row in enumerate(rows):
            cand_scr[r:r + 1, :] = row
        return _top_values(cand_scr[...], PEER_TOPK)

    for h in range(PEER_HEADS):
        e = []
        tops = []
        for p in range(2):
            r0 = (h * 2 + p) * PEER_HALF
            s = lax.dot_general(keys_ref[h, p], q_ref[:, r0:r0 + PEER_HALF], NT_DIMS, precision=HIGHEST,
                                preferred_element_type=F32)
            top = _top_values(s, PEER_TOPK)
            e.append(jnp.exp(s - top[0]))
            tops.append([jnp.exp(t - top[0]) for t in top])
        z = sum(top_candidates([tops[0][a] * tops[1][b] for a, b in PEER_PAIRS]))
        zinv = 1.0 / z
        p1 = e[0] * zinv
        tau = top_candidates([(tops[0][a] * zinv) * tops[1][b] for a, b in PEER_PAIRS])[-1]
        p1_ref[h] = p1
        e2_ref[h] = e[1]
        tau_ref[h] = jnp.broadcast_to(tau, (8, tau.shape[1]))


def _gelu_tanh(x):
    return 0.5 * x * (1.0 + jnp.tanh(float(np.sqrt(2.0 / np.pi)) * (x + 0.044715 * (x * x * x))))


def _peer_dense_kernel(h_ref, u_ref, vt_ref, p1_ref, e2_ref, tau_ref, o_ref, coef_scr):
    @pl.when(pl.program_id(1) == 0)
    def _():
        o_ref[...] = jnp.zeros_like(o_ref)

    act = _gelu_tanh(lax.dot_general(u_ref[...], h_ref[...], NT_DIMS, preferred_element_type=F32).astype(BF16))
    tb = h_ref.shape[0]
    for ts in range(tb // 128):
        ls = slice(ts * 128, (ts + 1) * 128)
        for i in range(PEER_IT):
            gate = jnp.zeros((PEER_NKEYS, 128), F32)
            for h in range(PEER_HEADS):
                w = e2_ref[h, :, ls] * p1_ref[h, i:i + 1, ls]
                gate = gate + jnp.where(w >= tau_ref[h, 0:1, ls], w, 0.0)
            rs = slice(i * PEER_NKEYS, (i + 1) * PEER_NKEYS)
            coef_scr[rs, ls] = gate.astype(BF16) * act[rs, ls]
    o_ref[...] += jnp.dot(vt_ref[...], coef_scr[...], preferred_element_type=F32)


def _peer_residual_kernel(x_ref, yt_ref, gain_ref, gate_ref, o_ref):
    o_ref[...] = x_ref[...] + gate_ref[0] * (_rms_rows(yt_ref[...].T) * gain_ref[...])


def peer_residual(x, q, h, keys, u, v_t, gain, gate):
    nseg, seg_len, d = x.shape
    n = nseg * seg_len
    nb = n // PEER_TB
    p1, e2, tau = pl.pallas_call(
        _peer_route_kernel,
        grid=(n // PEER_ROUTE_TB,),
        in_specs=[pl.BlockSpec((PEER_ROUTE_TB, PEER_HEADS * 2 * PEER_HALF), lambda i: (i, 0)),
                  pl.BlockSpec((PEER_HEADS, 2, PEER_NKEYS, PEER_HALF), lambda i: (0, 0, 0, 0))],
        out_specs=[pl.BlockSpec((PEER_HEADS, PEER_NKEYS, PEER_ROUTE_TB), lambda i: (0, 0, i)),
                   pl.BlockSpec((PEER_HEADS, PEER_NKEYS, PEER_ROUTE_TB), lambda i: (0, 0, i)),
                   pl.BlockSpec((PEER_HEADS, 8, PEER_ROUTE_TB), lambda i: (0, 0, i))],
        out_shape=[jax.ShapeDtypeStruct((PEER_HEADS, PEER_NKEYS, n), F32),
                   jax.ShapeDtypeStruct((PEER_HEADS, PEER_NKEYS, n), F32),
                   jax.ShapeDtypeStruct((PEER_HEADS, 8, n), F32)],
        scratch_shapes=[pltpu.VMEM((PEER_CAND_ROWS, PEER_ROUTE_TB), F32)],
        compiler_params=_params("parallel"),
        name="peer_route",
    )(q.reshape(n, -1), keys)
    n_tiles = N_EXPERTS // PEER_ET
    y_t = pl.pallas_call(
        _peer_dense_kernel,
        grid=(nb, n_tiles),
        in_specs=[pl.BlockSpec((PEER_TB, d), lambda i, e: (i, 0)),
                  pl.BlockSpec((PEER_ET, d), lambda i, e: (e, 0)),
                  pl.BlockSpec((d, PEER_ET), lambda i, e: (0, e)),
                  pl.BlockSpec((PEER_HEADS, PEER_IT, PEER_TB), lambda i, e: (0, e, i)),
                  pl.BlockSpec((PEER_HEADS, PEER_NKEYS, PEER_TB), lambda i, e: (0, 0, i)),
                  pl.BlockSpec((PEER_HEADS, 8, PEER_TB), lambda i, e: (0, 0, i))],
        out_specs=pl.BlockSpec((d, PEER_TB), lambda i, e: (0, i)),
        out_shape=jax.ShapeDtypeStruct((d, n), F32),
        scratch_shapes=[pltpu.VMEM((PEER_ET, PEER_TB), BF16)],
        compiler_params=_params("parallel", "arbitrary"),
        name="peer_dense",
    )(h.reshape(n, d), u, v_t, p1, e2, tau)
    tr = 256
    per_seg = seg_len // tr
    return pl.pallas_call(
        _peer_residual_kernel,
        grid=(n // tr,),
        in_specs=[pl.BlockSpec((tr, d), lambda i: (i, 0)),
                  pl.BlockSpec((d, tr), lambda i: (0, i)),
                  pl.BlockSpec((1, d), lambda i: (0, 0)),
                  pl.BlockSpec((1, 1, d), lambda i: (i // per_seg, 0, 0))],
        out_specs=pl.BlockSpec((tr, d), lambda i: (i, 0)),
        out_shape=jax.ShapeDtypeStruct((n, d), F32),
        compiler_params=_params("parallel"),
        name="peer_residual",
    )(x.reshape(n, d), y_t, gain, gate).reshape(nseg, seg_len, d)


def _rmsnorm(x, g):
    return x * lax.rsqrt(jnp.mean(x * x, axis=-1, keepdims=True) + EPS) * g


def _dft_tables(t):
    c = 1 << (int(np.log2(t)) // 2)
    r = t // c
    k = jnp.arange(t, dtype=jnp.int32)

    def angles(rows):
        return ((rows[:, None] * k[None, :]) % t).astype(F32) * (2.0 * np.pi / t)

    hi = angles(jnp.arange(r, dtype=jnp.int32) * c)[:, None, :]
    lo = angles(jnp.arange(c, dtype=jnp.int32))[None, :, :]
    cos = jnp.cos(hi) * jnp.cos(lo) - jnp.sin(hi) * jnp.sin(lo)
    sin = jnp.sin(hi) * jnp.cos(lo) + jnp.cos(hi) * jnp.sin(lo)
    return cos.reshape(t, t), sin.reshape(t, t)


def fnet_mix(xf, n_ctx_seq, ctx_len):
    nseg, seg_len, _ = xf.shape
    cc, sc = _dft_tables(FNET_CH)
    eye = jnp.eye(FNET_GROUPS, dtype=F32)
    w_ch = jnp.concatenate([jnp.kron(eye, cc), jnp.kron(eye, sc)], axis=1)
    z = matmul(xf.reshape(nseg * seg_len, FNET_WIDTH), w_ch).reshape(nseg, seg_len, 2 * FNET_WIDTH)

    def seq_dft(zz, t):
        bsz = zz.shape[0]
        ct, st = _dft_tables(t)
        scale = (t * FNET_CH) ** -0.5
        w_t = jnp.concatenate([ct, -st], axis=1) * scale
        stacked = jnp.concatenate([zz[..., :FNET_WIDTH], zz[..., FNET_WIDTH:]], axis=1)
        rhs = jnp.swapaxes(stacked, 0, 1).reshape(2 * t, bsz * FNET_WIDTH)
        y = matmul(w_t.astype(BF16), rhs.astype(BF16), tm=1024, tn=1024)
        return jnp.swapaxes(y.reshape(t, bsz, FNET_WIDTH), 0, 1)

    y_ctx = seq_dft(z[0].reshape(n_ctx_seq, ctx_len, 2 * FNET_WIDTH), ctx_len).reshape(1, seg_len, FNET_WIDTH)
    y_lat = seq_dft(z[1:], seg_len)
    return jnp.concatenate([y_ctx, y_lat], axis=0)


def _axial_rope(x):
    t = x.shape[1]
    rows = t // GRID_W
    row = jnp.repeat(jnp.arange(rows, dtype=F32), GRID_W)
    col = jnp.tile(jnp.arange(GRID_W, dtype=F32), rows)
    freqs = ROPE_THETA ** (-jnp.arange(AXIS_DIM // 2, dtype=F32) * 2.0 / AXIS_DIM)

    def rot(xa, pos):
        ang = pos[:, None] * freqs[None, :]
        cos, sin = jnp.cos(ang)[None, :, None, :], jnp.sin(ang)[None, :, None, :]
        x1, x2 = xa[..., :AXIS_DIM // 2], xa[..., AXIS_DIM // 2:]
        return jnp.concatenate([x1 * cos - x2 * sin, x2 * cos + x1 * sin], axis=-1)

    return jnp.concatenate([rot(x[..., :AXIS_DIM], row), rot(x[..., AXIS_DIM:], col)], axis=-1)


EVEN_COLS = (GLA_HEADS * GLA_DK, GLA_HEADS * GLA_DK, GLA_WIDTH, GLA_WIDTH, FNET_WIDTH, GLA_GATE_RANK, GLA_GATE_RANK)
ODD_COLS = (ATT_WIDTH, GDN_WIDTH, GDN_WIDTH, GDN_WIDTH, GDN_WIDTH, KV_WIDTH, KV_WIDTH,
            GDN_HEADS, GDN_HEADS, GDN_HEADS, GDN_HEADS)
PROJ_TN = 768


def _pad_cols(w):
    return jnp.pad(w, ((0, 0), (0, _round_up(w.shape[1], PROJ_TN) - w.shape[1])))


def _even_weight(w_in):
    q, k, v, g, lr_f, lr_b, xf = _split_cols(w_in, EVEN_SPLITS)
    return _pad_cols(jnp.concatenate([q, k, v, g, xf, lr_f, lr_b], axis=1)).astype(BF16)


def _odd_weight(w_in):
    qa, ka, va, qkv, z, b_f, b_b, a_f, a_b = _split_cols(w_in, ODD_SPLITS)
    return _pad_cols(jnp.concatenate([qa, qkv, z, ka, va, b_f, b_b, a_f, a_b], axis=1)).astype(BF16)


def _even_mix(proj, n_ctx, ctx_len, gate_w2, gate_b, state):
    nseg, seg_len, width = proj.shape
    xf = proj[..., sum(EVEN_COLS[:4]):sum(EVEN_COLS[:5])]
    zeros = jnp.zeros_like(gate_w2[0])
    w2 = jnp.stack([jnp.concatenate([gate_w2[0], zeros]), jnp.concatenate([zeros, gate_w2[1]])])
    gb = gate_b[:, None, :]
    zero = jnp.zeros((n_ctx, 2, GLA_HEADS, GLA_DK, GLA_DV), F32)
    of_c, ob_c, s_ctx = gla_scan(proj.reshape(nseg * n_ctx, ctx_len, width), 0, n_ctx, w2, gb, zero)
    o_f, o_b, _ = gla_scan(proj, 1, nseg - 1, w2, gb, state, into=(of_c, ob_c))
    return o_f, o_b, fnet_mix(xf, n_ctx, ctx_len), s_ctx


def _odd_mix(proj, n_ctx, ctx_len, qk_g, conv_w, a_log, dt_bias, ctx_k, ctx_v, state):
    nseg, seg_len, width = proj.shape
    ka, va = _split_cols(proj[..., sum(ODD_COLS[:5]):sum(ODD_COLS[:7])], ODD_COLS[5:7])
    proj_ctx = proj.reshape(nseg * n_ctx, ctx_len, width)
    q_gain = qk_g[0][None, :]

    def keys_values(sl, bsz, t, latent):
        k4 = _rmsnorm(ka[sl].reshape(bsz, t, ATT_KV_HEADS, ATT_HD), qk_g[1])
        v4 = va[sl].reshape(bsz, t, ATT_KV_HEADS, ATT_HD)
        if latent:
            k_all = jnp.concatenate([ctx_k, _axial_rope(k4)], axis=1)
            v_all = jnp.concatenate([ctx_v, v4], axis=1)
        else:
            k_all, v_all = k4, v4
        t_k = k_all.shape[1]
        return (k_all.reshape(bsz, t_k, KV_WIDTH).astype(BF16), v_all.reshape(bsz, t_k, KV_WIDTH).astype(BF16),
                k4, v4)

    kc, vc, k_ctx, v_ctx = keys_values(slice(0, 1), n_ctx, ctx_len, False)
    kl, vl, _, _ = keys_values(slice(1, None), nseg - 1, seg_len, True)
    att_c = attention(proj_ctx, 0, n_ctx, q_gain, kc, vc, rope=False)
    att = attention(proj, 1, nseg - 1, q_gain, kl, vl, rope=True, into=(att_c,))
    zero = jnp.zeros((n_ctx, 2, GDN_HEADS, GDN_DK, GDN_DV), F32)
    of_c, ob_c, s_ctx = gdn_scan(proj_ctx, 0, n_ctx, conv_w, a_log, dt_bias, zero)
    o_f, o_b, _ = gdn_scan(proj, 1, nseg - 1, conv_w, a_log, dt_bias, state, into=(of_c, ob_c))
    return o_f, o_b, att, k_ctx, v_ctx, s_ctx


def kernel(x_prompt, x_sample, cache_k, cache_v, state_gla, state_delta, c, c_ctx, ada_w, ada_b, norm_g, w_out, even_w_in, gla_gate_w2, gla_gate_b, gla_out_norm, odd_w_in, qk_norm, conv_w, gdn_a_log, gdn_dt_bias, gdn_out_norm, peer_wq, peer_keys, peer_u, peer_v):
    n_ctx, ctx_len, d = x_prompt.shape
    n_lat, seg_len, _ = x_sample.shape
    assert n_ctx * ctx_len == seg_len and d == D_MODEL
    depth = ada_w.shape[0]
    x = jnp.concatenate([x_prompt.reshape(1, seg_len, d), x_sample], axis=0)
    cvec = jnp.concatenate([c_ctx[None, :], c], axis=0)
    new_gla, new_delta, new_k, new_v = [], [], [], []
    for l in range(depth):
        i = l // 2
        mod = matmul(jax.nn.silu(cvec), ada_w[l], tn=1024) + ada_b[l]
        sh1, sc1, g1, sh2, sc2, g2 = [m[:, None, :] for m in jnp.split(mod, 6, axis=-1)]
        gains = norm_g[l][:, None, :]
        w_o = w_out[l].astype(BF16)
        if l % 2 == 0:
            proj = norm_matmul(x, gains[0], 1 + sc1, sh1, _even_weight(even_w_in[i]), tm=1024, tn=PROJ_TN)
            o_f, o_b, other, s_new = _even_mix(proj, n_ctx, ctx_len, gla_gate_w2[i], gla_gate_b[i], state_gla[:, i])
            new_gla.append(s_new)
            x = proj_residual(o_f, o_b, proj, gla_out_norm[i][None, :], other, w_o, x, gains[1], g1,
                              heads=GLA_HEADS, hd=GLA_DV, z_block=2, normed_first=True)
        else:
            proj = norm_matmul(x, gains[0], 1 + sc1, sh1, _odd_weight(odd_w_in[i]), tm=1024, tn=PROJ_TN)
            o_f, o_b, other, k_new, v_new, s_new = _odd_mix(proj, n_ctx, ctx_len, qk_norm[i], conv_w[i], gdn_a_log[i],
                                                            gdn_dt_bias[i], cache_k[:, i], cache_v[:, i],
                                                            state_delta[:, i])
            new_k.append(k_new)
            new_v.append(v_new)
            new_delta.append(s_new)
            x = proj_residual(o_f, o_b, proj, gdn_out_norm[i][None, :], other, w_o, x, gains[1], g1,
                              heads=GDN_HEADS, hd=GDN_DV, z_block=4, normed_first=False)
        q, h = norm_matmul(x, gains[2], 1 + sc2, sh2, peer_wq[l].astype(BF16), tm=512, tn=1024, emit_h=True)
        x = peer_residual(x, q, h, peer_keys[l], peer_u[l].astype(BF16), peer_v[l].T.astype(BF16), gains[3], g2)
    return (x[0].reshape(n_ctx, ctx_len, d), x[1:],
            jnp.stack(new_k, axis=1), jnp.stack(new_v, axis=1),
            jnp.stack(new_gla, axis=1), jnp.stack(new_delta, axis=1))
```
